```python
import jax, jax.numpy as jnp
from jax import lax
import numpy as np

D_MODEL = 2048
BATCH = 16
SEQ = 256
DEPTH = 4
DEC_BATCH = 4
DEC_SEQ = 2048
PAST_LEN = 512

GRID_W = 64
D_MIX = D_MODEL
N_HEADS = D_MODEL // 256
QK_NOPE = 128
ROPE_DIM = 64
QK_DIM = QK_NOPE + ROPE_DIM
V_DIM = 128
Q_LORA = D_MODEL // 4
KV_LORA = D_MODEL // 8
ATT_W = N_HEADS * V_DIM
POOL_WINDOWS = (2, 4, 8, 16)
POOL_GROUPS = 4
POOL_W = D_MIX // 4
POOL_GC = POOL_W // POOL_GROUPS
CONV_W = D_MIX // 4
CONV_K = 3
IN_W = Q_LORA + KV_LORA + ROPE_DIM + POOL_W + 3 * CONV_W
D_FF = -(-8 * D_MODEL // (3 * 256)) * 256
Q_BLOCK = 128
ROPE_BASE = 10000.0
EPS = 1e-6

kernel_name = "hybrid_mla_pool_conv_flow_step"


def _rmsnorm(x, g):
    xf = x.astype(jnp.float32)
    y = xf * lax.rsqrt(jnp.mean(xf * xf, axis=-1, keepdims=True) + EPS)
    return (y * g.astype(jnp.float32)).astype(x.dtype)


def _rope_2d(x, n_tokens):
    rows = n_tokens // GRID_W
    r_pos = jnp.repeat(jnp.arange(rows), GRID_W).astype(jnp.float32)
    c_pos = jnp.tile(jnp.arange(GRID_W), rows).astype(jnp.float32)
    half = ROPE_DIM // 2
    nf = half // 2
    inv = ROPE_BASE ** (-jnp.arange(nf, dtype=jnp.float32) / nf)

    def rot(xa, pos):
        ang = pos[:, None] * inv[None, :]
        cos = jnp.cos(ang)[:, None, :]
        sin = jnp.sin(ang)[:, None, :]
        x1, x2 = xa[..., :nf], xa[..., nf:]
        return jnp.concatenate([x1 * cos - x2 * sin, x2 * cos + x1 * sin], axis=-1)

    xf = x.astype(jnp.float32)
    out = jnp.concatenate([rot(xf[..., :half], r_pos), rot(xf[..., half:], c_pos)], axis=-1)
    return out.astype(x.dtype)


def _mla_kv(ckv, krope, w_ukv, g_kn):
    Bn, L, _ = ckv.shape
    kv = (ckv @ w_ukv).reshape(Bn, L, N_HEADS, QK_NOPE + V_DIM)
    k_nope, v = kv[..., :QK_NOPE], kv[..., QK_NOPE:]
    k_r = jnp.broadcast_to(krope[:, :, None, :], (Bn, L, N_HEADS, ROPE_DIM))
    k = _rmsnorm(jnp.concatenate([k_nope, k_r], axis=-1), g_kn)
    return k, v


def _attend(q, k, v):
    Bn, Lq, H, Dk = q.shape
    nb = Lq // Q_BLOCK
    kt = k.transpose(0, 2, 1, 3)
    vt = v.transpose(0, 2, 1, 3)
    qb = q.reshape(Bn, nb, Q_BLOCK, H, Dk).transpose(1, 0, 3, 2, 4)
    scale = QK_DIM ** -0.5

    def block(qi):
        s = jnp.einsum("bhqd,bhkd->bhqk", qi, kt).astype(jnp.float32) * scale
        p = jax.nn.softmax(s, axis=-1)
        return jnp.einsum("bhqk,bhkd->bhqd", p.astype(vt.dtype), vt)

    o = lax.map(block, qb)
    return o.transpose(1, 0, 3, 2, 4).reshape(Bn, Lq, H * V_DIM)


def _multiscale_pool(u, w_pool, pool_scale):
    Bn, L, _ = u.shape
    ug = u.reshape(Bn, L, POOL_GROUPS, POOL_GC)
    uf = ug.astype(jnp.float32)
    cs = jnp.concatenate([jnp.zeros((Bn, 1, POOL_GROUPS, POOL_GC), jnp.float32),
                          jnp.cumsum(uf, axis=1)], axis=1)
    t = jnp.arange(L)
    outs = []
    for g, w in enumerate(POOL_WINDOWS):
        lo = jnp.maximum(t - w // 2, 0)
        hi = jnp.minimum(t + w // 2, L)
        mean = (cs[:, hi, g] - cs[:, lo, g]) / (hi - lo).astype(jnp.float32)[None, :, None]
        outs.append(mean - uf[:, :, g])
    d = jnp.stack(outs, axis=2).astype(u.dtype)
    y = jnp.einsum("blgc,gcd->blgd", d, w_pool)
    return y.reshape(Bn, L, POOL_W) * pool_scale


def _dwconv3(z, conv_w):
    zp = jnp.pad(z, ((0, 0), (1, 1), (0, 0)))
    return zp[:, :-2] * conv_w[0] + zp[:, 1:-1] * conv_w[1] + zp[:, 2:] * conv_w[2]


def _mixers(h, l, P, ctx):
    Bn, L, _ = h.shape
    u = h @ P["w_in"][l]
    o1 = Q_LORA
    o2 = o1 + KV_LORA
    o3 = o2 + ROPE_DIM
    o4 = o3 + POOL_W
    o5 = o4 + CONV_W
    o6 = o5 + CONV_W
    cq, ckv, krope, pin, ch, cb, cc = jnp.split(u, [o1, o2, o3, o4, o5, o6], axis=-1)
    q = (_rmsnorm(cq, P["g_q_a"][l]) @ P["w_uq"][l]).reshape(Bn, L, N_HEADS, QK_DIM)
    q = _rmsnorm(q, P["g_qn"][l])
    ckv = _rmsnorm(ckv, P["g_kv_a"][l])
    k, v = _mla_kv(ckv, krope, P["w_ukv"][l], P["g_kn"][l])
    if ctx is None:
        att = _attend(q, k, v)
        new = (ckv, krope)
    else:
        ckv_c, kr_c = ctx
        q = jnp.concatenate([q[..., :QK_NOPE], _rope_2d(q[..., QK_NOPE:], L)], axis=-1)
        k = jnp.concatenate([k[..., :QK_NOPE], _rope_2d(k[..., QK_NOPE:], L)], axis=-1)
        k_c, v_c = _mla_kv(ckv_c, kr_c, P["w_ukv"][l], P["g_kn"][l])
        att = _attend(q, jnp.concatenate([k, k_c], axis=1), jnp.concatenate([v, v_c], axis=1))
        new = None
    pool = _multiscale_pool(pin, P["w_pool"][l], P["pool_scale"][l])
    conv = cb * _dwconv3(cc * ch, P["conv_w"][l])
    out = jnp.concatenate([att, pool, conv], axis=-1) @ P["w_out"][l]
    return out, new


def _layer(x, m, l, P, ctx):
    mod = (jax.nn.silu(m) @ P["w_ada"][l] + P["b_ada"][l])[:, None, :]
    sh1, sc1, g1, sh2, sc2, g2 = jnp.split(mod, 6, axis=-1)
    h = _rmsnorm(x, P["g_mix"][l]) * (1 + sc1) + sh1
    mix, new = _mixers(h, l, P, ctx)
    x = x + g1 * mix
    h = _rmsnorm(x, P["g_ffn"][l]) * (1 + sc2) + sh2
    f = (jax.nn.silu(h @ P["w_gate"][l]) * (h @ P["w_up"][l])) @ P["w_down"][l]
    return x + g2 * f, new


def setup_inputs(seed: int = 0) -> dict:
    key = jax.random.key(seed)
    ks = jax.random.split(key, 24)
    f32 = jnp.float32
    nrm = lambda k, s, sc: jax.random.normal(k, s, f32) * sc
    gain = lambda k, s: 1.0 + 0.1 * jax.random.normal(k, s, f32)
    return {
        "x_prompt": nrm(ks[0], (BATCH, SEQ, D_MODEL), 1.0),
        "x_sample": nrm(ks[1], (DEC_BATCH, DEC_SEQ, D_MODEL), 1.0),
        "cache_ckv": nrm(ks[2], (DEC_BATCH, DEPTH, PAST_LEN, KV_LORA), 1.0),
        "cache_krope": nrm(ks[3], (DEC_BATCH, DEPTH, PAST_LEN, ROPE_DIM), 1.0),
        "c": nrm(ks[4], (DEC_BATCH, D_MODEL), 1.0),
        "c_ctx": nrm(ks[5], (D_MODEL,), 1.0),
        "w_ada": nrm(ks[6], (DEPTH, D_MODEL, 6 * D_MODEL), D_MODEL ** -0.5),
        "b_ada": nrm(ks[7], (DEPTH, 6 * D_MODEL), 0.02),
        "g_mix": gain(ks[8], (DEPTH, D_MODEL)),
        "g_ffn": gain(ks[9], (DEPTH, D_MODEL)),
        "w_in": nrm(ks[10], (DEPTH, D_MODEL, IN_W), D_MODEL ** -0.5),
        "g_q_a": gain(ks[11], (DEPTH, Q_LORA)),
        "w_uq": nrm(ks[12], (DEPTH, Q_LORA, N_HEADS * QK_DIM), Q_LORA ** -0.5),
        "g_kv_a": gain(ks[13], (DEPTH, KV_LORA)),
        "w_ukv": nrm(ks[14], (DEPTH, KV_LORA, N_HEADS * (QK_NOPE + V_DIM)), KV_LORA ** -0.5),
        "g_qn": gain(ks[15], (DEPTH, QK_DIM)),
        "g_kn": gain(ks[16], (DEPTH, QK_DIM)),
        "w_pool": nrm(ks[17], (DEPTH, POOL_GROUPS, POOL_GC, POOL_GC), POOL_GC ** -0.5),
        "pool_scale": gain(ks[18], (DEPTH, POOL_W)),
        "conv_w": nrm(ks[19], (DEPTH, CONV_K, CONV_W), CONV_K ** -0.5),
        "w_out": nrm(ks[20], (DEPTH, D_MIX, D_MODEL), D_MIX ** -0.5),
        "w_gate": nrm(ks[21], (DEPTH, D_MODEL, D_FF), D_MODEL ** -0.5),
        "w_up": nrm(ks[22], (DEPTH, D_MODEL, D_FF), D_MODEL ** -0.5),
        "w_down": nrm(ks[23], (DEPTH, D_FF, D_MODEL), D_FF ** -0.5),
    }


def reference(x_prompt, x_sample, cache_ckv, cache_krope, c, c_ctx, w_ada, b_ada, g_mix,
              g_ffn, w_in, g_q_a, w_uq, g_kv_a, w_ukv, g_qn, g_kn, w_pool, pool_scale,
              conv_w, w_out, w_gate, w_up, w_down):
    P = dict(w_ada=w_ada, b_ada=b_ada, g_mix=g_mix, g_ffn=g_ffn, w_in=w_in, g_q_a=g_q_a,
             w_uq=w_uq, g_kv_a=g_kv_a, w_ukv=w_ukv, g_qn=g_qn, g_kn=g_kn, w_pool=w_pool,
             pool_scale=pool_scale, conv_w=conv_w, w_out=w_out, w_gate=w_gate, w_up=w_up,
             w_down=w_down)
    y_prompt = x_prompt
    m_ctx = c_ctx[None, :]
    ckv_list = []
    kr_list = []
    for l in range(DEPTH):
        y_prompt, (ckv_l, kr_l) = _layer(y_prompt, m_ctx, l, P, None)
        ckv_list.append(ckv_l)
        kr_list.append(kr_l)
    new_ckv = jnp.stack(ckv_list, axis=1)
    new_krope = jnp.stack(kr_list, axis=1)
    y_sample = x_sample
    for l in range(DEPTH):
        y_sample, _ = _layer(y_sample, c, l, P, (cache_ckv[:, l], cache_krope[:, l]))
    return (y_prompt, y_sample, new_ckv, new_krope)
```

```python
import functools

import jax
import jax.numpy as jnp
from jax import lax
from jax.experimental import pallas as pl
from jax.experimental.pallas import tpu as pltpu

D_MODEL = 2048
BATCH = 16
SEQ = 256
DEPTH = 4
DEC_BATCH = 4
DEC_SEQ = 2048
PAST_LEN = 512
GRID_W = 64
N_HEADS = 8
QK_NOPE = 128
ROPE_DIM = 64
QK_DIM = QK_NOPE + ROPE_DIM
V_DIM = 128
Q_LORA = 512
KV_LORA = 256
POOL_WINDOWS = (2, 4, 8, 16)
POOL_GC = 128
POOL_W = 512
CONV_W = 512
D_FF = 5632
ROPE_BASE = 10000.0
EPS = 1e-6

N_CTX = BATCH * SEQ
N_DEC = DEC_BATCH * DEC_SEQ
N_TOK = N_CTX + N_DEC
N_MOD = 6 * D_MODEL
MOD_ROWS = 8

LANE = 128
HEAD_PAD = 2 * LANE
IN_CQ = 0
IN_CKV = IN_CQ + Q_LORA
IN_KR = IN_CKV + KV_LORA
IN_PIN = IN_KR + LANE
IN_CONV = IN_PIN + POOL_W
IN_PAD = IN_CONV + 3 * CONV_W

TM_IN = 256
TM_OUT = 512
TM_FFN = 512
TF_FFN = 512
TN_ADA = 1024
TS_MIX = 2048
TQ_DEC = 512
VMEM_LIMIT = 56 * 1024 * 1024

F32 = jnp.float32
BF16 = jnp.bfloat16


def _mod_row(i, tm):
    n_ctx_tiles = N_CTX // tm
    return jnp.where(i < n_ctx_tiles, 0, 1 + (i - n_ctx_tiles) // (DEC_SEQ // tm))


def _params(sem, vmem=VMEM_LIMIT):
    return pltpu.CompilerParams(dimension_semantics=sem, vmem_limit_bytes=vmem)


def _silu(x):
    return x / (1.0 + jnp.exp(-x))


def _rms(x, n):
    return lax.rsqrt(jnp.sum(x * x, axis=-1, keepdims=True) * (1.0 / n) + EPS)


def _ada_kernel(m_ref, w_ref, b_ref, o_ref):
    a = _silu(m_ref[...]).astype(BF16)
    o_ref[...] = jnp.dot(a, w_ref[...].astype(BF16), preferred_element_type=F32) + b_ref[...]


def _ada(m_all, w_ada, b_ada):
    return pl.pallas_call(
        _ada_kernel,
        grid=(DEPTH, N_MOD // TN_ADA),
        in_specs=[
            pl.BlockSpec((MOD_ROWS, D_MODEL), lambda l, j: (0, 0)),
            pl.BlockSpec((None, D_MODEL, TN_ADA), lambda l, j: (l, 0, j)),
            pl.BlockSpec((None, 1, TN_ADA), lambda l, j: (l, 0, j)),
        ],
        out_specs=pl.BlockSpec((None, MOD_ROWS, TN_ADA), lambda l, j: (l, 0, j)),
        out_shape=jax.ShapeDtypeStruct((DEPTH, MOD_ROWS, N_MOD), F32),
        compiler_params=_params(("parallel", "parallel")),
        name="ada_mod",
    )(m_all, w_ada, b_ada.reshape(DEPTH, 1, N_MOD))


def _rope_tables():
    t = jnp.arange(DEC_SEQ)
    r_pos = (t // GRID_W).astype(F32)
    c_pos = (t % GRID_W).astype(F32)
    nf = ROPE_DIM // 4
    inv = ROPE_BASE ** (-jnp.arange(nf, dtype=F32) / nf)
    ang_r = r_pos[:, None] * inv[None, :]
    ang_c = c_pos[:, None] * inv[None, :]
    zeros = jnp.zeros((DEC_SEQ, LANE - ROPE_DIM), F32)
    cos = jnp.concatenate([jnp.cos(ang_r), jnp.cos(ang_r), jnp.cos(ang_c), jnp.cos(ang_c), zeros], axis=1)
    sin = jnp.concatenate([-jnp.sin(ang_r), jnp.sin(ang_r), -jnp.sin(ang_c), jnp.sin(ang_c), zeros], axis=1)
    ident_cos = jnp.concatenate([jnp.ones((TM_IN, ROPE_DIM), F32), jnp.zeros((TM_IN, LANE - ROPE_DIM), F32)], axis=1)
    ident_sin = jnp.zeros((TM_IN, LANE), F32)
    return jnp.concatenate([ident_cos, cos], axis=0), jnp.concatenate([ident_sin, sin], axis=0)


def _rope(x, cos, sin, first_half):
    n = x.shape[-1]
    swapped = jnp.where(first_half, pltpu.roll(x, n - ROPE_DIM // 4, 1), pltpu.roll(x, ROPE_DIM // 4, 1))
    return x * cos + swapped * sin


def _in_kernel(x_ref, sh_ref, sc_ref, gmix_ref, win_ref, gqa_ref, wuq_ref, gqn_ref, gkva_ref, wukv_ref,
               gkn_ref, cos_ref, sin_ref, q_ref, k_ref, v_ref, ckv_ref, kr_ref, pin_ref, cv_ref):
    x = x_ref[...]
    h = x * _rms(x, D_MODEL) * gmix_ref[...]
    h = h * (1.0 + sc_ref[...]) + sh_ref[...]
    u = jnp.dot(h.astype(BF16), win_ref[...], preferred_element_type=F32)

    cos = cos_ref[...]
    sin = sin_ref[...]
    lane = lax.broadcasted_iota(jnp.int32, cos.shape, 1)
    first_half = (lane & (ROPE_DIM // 4)) == 0
    scale = QK_DIM ** -0.5

    cq = u[:, IN_CQ:IN_CQ + Q_LORA]
    cqn = cq * _rms(cq, Q_LORA) * gqa_ref[...]
    q = jnp.dot(cqn.astype(BF16), wuq_ref[...], preferred_element_type=F32)
    gqn = gqn_ref[...]
    for hd in range(N_HEADS):
        qh = q[:, hd * HEAD_PAD:(hd + 1) * HEAD_PAD]
        qh = qh * (_rms(qh, QK_DIM) * scale) * gqn
        q_ref[:, hd * HEAD_PAD:hd * HEAD_PAD + LANE] = qh[:, :LANE].astype(BF16)
        q_ref[:, hd * HEAD_PAD + LANE:(hd + 1) * HEAD_PAD] = _rope(qh[:, LANE:], cos, sin, first_half).astype(BF16)

    ckv = u[:, IN_CKV:IN_CKV + KV_LORA]
    ckvn = ckv * _rms(ckv, KV_LORA) * gkva_ref[...]
    ckv_ref[...] = ckvn
    kv = jnp.dot(ckvn.astype(BF16), wukv_ref[...], preferred_element_type=F32)
    krb = u[:, IN_KR:IN_KR + LANE]
    kr_ref[...] = krb[:, :ROPE_DIM]
    kr_ss = jnp.sum(krb * krb, axis=-1, keepdims=True)
    gkn = gkn_ref[...]
    krg = _rope(krb * gkn[:, LANE:], cos, sin, first_half)
    for hd in range(N_HEADS):
        kn = kv[:, hd * HEAD_PAD:hd * HEAD_PAD + LANE]
        rs = lax.rsqrt((jnp.sum(kn * kn, axis=-1, keepdims=True) + kr_ss) * (1.0 / QK_DIM) + EPS)
        k_ref[:, hd * HEAD_PAD:hd * HEAD_PAD + LANE] = (kn * rs * gkn[:, :LANE]).astype(BF16)
        k_ref[:, hd * HEAD_PAD + LANE:(hd + 1) * HEAD_PAD] = (krg * rs).astype(BF16)
        v_ref[:, hd * V_DIM:(hd + 1) * V_DIM] = kv[:, hd * HEAD_PAD + LANE:(hd + 1) * HEAD_PAD].astype(BF16)

    pin_ref[...] = u[:, IN_PIN:IN_PIN + POOL_W]
    cv_ref[...] = u[:, IN_CONV:IN_CONV + 3 * CONV_W]


def _in_proj(l, x_all, mod4, W):
    tm = TM_IN
    n_ctx_tiles = N_CTX // tm
    row = functools.partial(_mod_row, tm=tm)
    vec = lambda w: pl.BlockSpec((None, 1, w), lambda i: (l, 0, 0))
    mat = lambda r, c: pl.BlockSpec((None, r, c), lambda i: (l, 0, 0), pipeline_mode=pl.Buffered(1))
    tab = pl.BlockSpec((tm, LANE), lambda i: (jnp.where(i < n_ctx_tiles, 0, 1 + (i - n_ctx_tiles) % (DEC_SEQ // tm)), 0))
    tok = lambda w: pl.BlockSpec((tm, w), lambda i: (i, 0))
    return pl.pallas_call(
        _in_kernel,
        grid=(N_TOK // tm,),
        in_specs=[
            tok(D_MODEL),
            pl.BlockSpec((None, None, 1, D_MODEL), lambda i: (l, row(i), 0, 0)),
            pl.BlockSpec((None, None, 1, D_MODEL), lambda i: (l, row(i), 0, 1)),
            vec(D_MODEL), mat(D_MODEL, IN_PAD), vec(Q_LORA), mat(Q_LORA, N_HEADS * HEAD_PAD), vec(HEAD_PAD),
            vec(KV_LORA), mat(KV_LORA, N_HEADS * HEAD_PAD), vec(HEAD_PAD), tab, tab,
        ],
        out_specs=[tok(N_HEADS * HEAD_PAD), tok(N_HEADS * HEAD_PAD), tok(N_HEADS * V_DIM), tok(KV_LORA),
                   tok(ROPE_DIM), tok(POOL_W), tok(3 * CONV_W)],
        out_shape=[
            jax.ShapeDtypeStruct((N_TOK, N_HEADS * HEAD_PAD), BF16),
            jax.ShapeDtypeStruct((N_TOK, N_HEADS * HEAD_PAD), BF16),
            jax.ShapeDtypeStruct((N_TOK, N_HEADS * V_DIM), BF16),
            jax.ShapeDtypeStruct((N_TOK, KV_LORA), F32),
            jax.ShapeDtypeStruct((N_TOK, ROPE_DIM), F32),
            jax.ShapeDtypeStruct((N_TOK, POOL_W), F32),
            jax.ShapeDtypeStruct((N_TOK, 3 * CONV_W), F32),
        ],
        compiler_params=_params(("parallel",)),
        name=f"in_proj_{l}",
    )(x_all, mod4, mod4, W["g_mix"], W["w_in"], W["g_q_a"], W["w_uq"], W["g_qn"], W["g_kv_a"], W["w_ukv"],
      W["g_kn"], W["cos"], W["sin"])


def _kvc_kernel(ckv_ref, kr_ref, wukv_ref, gkn_ref, k_ref, v_ref):
    kv = jnp.dot(ckv_ref[...].astype(BF16), wukv_ref[...], preferred_element_type=F32)
    krb = kr_ref[...]
    kr_ss = jnp.sum(krb * krb, axis=-1, keepdims=True)
    gkn = gkn_ref[...]
    krg = krb * gkn[:, LANE:]
    for hd in range(N_HEADS):
        kn = kv[:, hd * HEAD_PAD:hd * HEAD_PAD + LANE]
        rs = lax.rsqrt((jnp.sum(kn * kn, axis=-1, keepdims=True) + kr_ss) * (1.0 / QK_DIM) + EPS)
        k_ref[:, hd * HEAD_PAD:hd * HEAD_PAD + LANE] = (kn * rs * gkn[:, :LANE]).astype(BF16)
        k_ref[:, hd * HEAD_PAD + LANE:(hd + 1) * HEAD_PAD] = (krg * rs).astype(BF16)
        v_ref[:, hd * V_DIM:(hd + 1) * V_DIM] = kv[:, hd * HEAD_PAD + LANE:(hd + 1) * HEAD_PAD].astype(BF16)


def _cache_kv(cache_ckv, cache_kr_pad, W):
    return pl.pallas_call(
        _kvc_kernel,
        grid=(DEPTH, DEC_BATCH),
        in_specs=[
            pl.BlockSpec((None, None, PAST_LEN, KV_LORA), lambda l, b: (b, l, 0, 0)),
            pl.BlockSpec((None, None, PAST_LEN, LANE), lambda l, b: (b, l, 0, 0)),
            pl.BlockSpec((None, KV_LORA, N_HEADS * HEAD_PAD), lambda l, b: (l, 0, 0)),
            pl.BlockSpec((None, 1, HEAD_PAD), lambda l, b: (l, 0, 0)),
        ],
        out_specs=[
            pl.BlockSpec((None, None, PAST_LEN, N_HEADS * HEAD_PAD), lambda l, b: (l, b, 0, 0)),
            pl.BlockSpec((None, None, PAST_LEN, N_HEADS * V_DIM), lambda l, b: (l, b, 0, 0)),
        ],
        out_shape=[
            jax.ShapeDtypeStruct((DEPTH, DEC_BATCH, PAST_LEN, N_HEADS * HEAD_PAD), BF16),
            jax.ShapeDtypeStruct((DEPTH, DEC_BATCH, PAST_LEN, N_HEADS * V_DIM), BF16),
        ],
        compiler_params=_params(("parallel", "parallel")),
        name="cache_kv",
    )(cache_ckv, cache_kr_pad, W["w_ukv"], W["g_kn"])


def _softmax_pv(q, ks, vs):
    ss = [lax.dot_general(q, k, (((1,), (1,)), ((), ())), preferred_element_type=F32) for k in ks]
    m = functools.reduce(jnp.maximum, [jnp.max(s, axis=-1, keepdims=True) for s in ss])
    ps = [jnp.exp(s - m) for s in ss]
    den = functools.reduce(jnp.add, [jnp.sum(p, axis=-1, keepdims=True) for p in ps])
    o = functools.reduce(jnp.add, [jnp.dot(p.astype(BF16), v, preferred_element_type=F32) for p, v in zip(ps, vs)])
    return o / den


def _attn_ctx_kernel(q_ref, k_ref, v_ref, o_ref):
    for hd in range(N_HEADS):
        q = q_ref[:, hd * HEAD_PAD:(hd + 1) * HEAD_PAD]
        k = k_ref[:, hd * HEAD_PAD:(hd + 1) * HEAD_PAD]
        v = v_ref[:, hd * V_DIM:(hd + 1) * V_DIM]
        o_ref[:, hd * V_DIM:(hd + 1) * V_DIM] = _softmax_pv(q, [k], [v]).astype(BF16)


def _attn_ctx(l, q, k, v):
    return pl.pallas_call(
        _attn_ctx_kernel,
        grid=(BATCH,),
        in_specs=[
            pl.BlockSpec((SEQ, N_HEADS * HEAD_PAD), lambda b: (b, 0)),
            pl.BlockSpec((SEQ, N_HEADS * HEAD_PAD), lambda b: (b, 0)),
            pl.BlockSpec((SEQ, N_HEADS * V_DIM), lambda b: (b, 0)),
        ],
        out_specs=pl.BlockSpec((SEQ, N_HEADS * V_DIM), lambda b: (b, 0)),
        out_shape=jax.ShapeDtypeStruct((N_CTX, N_HEADS * V_DIM), BF16),
        compiler_params=_params(("parallel",)),
        name=f"attn_ctx_{l}",
    )(q, k, v)


def _attn_dec_kernel(q_ref, k_ref, v_ref, kc_ref, vc_ref, o_ref):
    k = k_ref[...]
    v = v_ref[...]
    kc = kc_ref[...]
    vc = vc_ref[...]

    def body(i, carry):
        r = pl.multiple_of(i * TQ_DEC, TQ_DEC)
        q = q_ref[pl.ds(r, TQ_DEC), :]
        o_ref[pl.ds(r, TQ_DEC), :] = _softmax_pv(q, [k, kc], [v, vc]).astype(BF16)
        return carry

    lax.fori_loop(0, DEC_SEQ // TQ_DEC, body, 0)


def _attn_dec(l, q, k, v, kc, vc):
    first = N_CTX // DEC_SEQ
    return pl.pallas_call(
        _attn_dec_kernel,
        grid=(DEC_BATCH, N_HEADS),
        in_specs=[
            pl.BlockSpec((DEC_SEQ, HEAD_PAD), lambda b, h: (first + b, h)),
            pl.BlockSpec((DEC_SEQ, HEAD_PAD), lambda b, h: (first + b, h)),
            pl.BlockSpec((DEC_SEQ, V_DIM), lambda b, h: (first + b, h)),
            pl.BlockSpec((None, None, PAST_LEN, HEAD_PAD), lambda b, h: (l, b, 0, h)),
            pl.BlockSpec((None, None, PAST_LEN, V_DIM), lambda b, h: (l, b, 0, h)),
        ],
        out_specs=pl.BlockSpec((DEC_SEQ, V_DIM), lambda b, h: (b, h)),
        out_shape=jax.ShapeDtypeStruct((N_DEC, N_HEADS * V_DIM), BF16),
        compiler_params=_params(("parallel", "parallel")),
        name=f"attn_dec_{l}",
    )(q, k, v, kc, vc)


def _mix_kernel(pin_ref, cv_ref, wp_ref, ps_ref, cw_ref, o_ref):
    i = pl.program_id(0)
    seq = jnp.where(i < N_CTX // TS_MIX, SEQ, DEC_SEQ)
    pos = lax.broadcasted_iota(jnp.int32, (TS_MIX, LANE), 0) & (seq - 1)

    def prev(x, d):
        return jnp.where(pos >= d, pltpu.roll(x, d, 0), 0.0)

    def nxt(x, d):
        return jnp.where(pos + d < seq, pltpu.roll(x, TS_MIX - d, 0), 0.0)

    for g, w in enumerate(POOL_WINDOWS):
        half = w // 2
        u = pin_ref[:, g * POOL_GC:(g + 1) * POOL_GC]
        fwd = u
        bwd = u
        span = 1
        while span < half:
            fwd = fwd + nxt(fwd, span)
            bwd = bwd + prev(bwd, span)
            span *= 2
        total = fwd + prev(bwd, 1)
        cnt = (jnp.minimum(pos + half, seq) - jnp.maximum(pos - half, 0)).astype(F32)
        d = total / cnt - u
        y = jnp.dot(d.astype(BF16), wp_ref[g], preferred_element_type=F32)
        o_ref[:, g * POOL_GC:(g + 1) * POOL_GC] = (y * ps_ref[:, g * POOL_GC:(g + 1) * POOL_GC]).astype(BF16)

    for g in range(CONV_W // LANE):
        sl = slice(g * LANE, (g + 1) * LANE)
        ch = cv_ref[:, g * LANE:(g + 1) * LANE]
        cb = cv_ref[:, CONV_W + g * LANE:CONV_W + (g + 1) * LANE]
        cc = cv_ref[:, 2 * CONV_W + g * LANE:2 * CONV_W + (g + 1) * LANE]
        z = cc * ch
        conv = prev(z, 1) * cw_ref[0:1, sl] + z * cw_ref[1:2, sl] + nxt(z, 1) * cw_ref[2:3, sl]
        o_ref[:, POOL_W + g * LANE:POOL_W + (g + 1) * LANE] = (cb * conv).astype(BF16)


def _mixers(l, pin, cv, W):
    return pl.pallas_call(
        _mix_kernel,
        grid=(N_TOK // TS_MIX,),
        in_specs=[
            pl.BlockSpec((TS_MIX, POOL_W), lambda i: (i, 0)),
            pl.BlockSpec((TS_MIX, 3 * CONV_W), lambda i: (i, 0)),
            pl.BlockSpec((None, len(POOL_WINDOWS), POOL_GC, POOL_GC), lambda i: (l, 0, 0, 0)),
            pl.BlockSpec((None, 1, POOL_W), lambda i: (l, 0, 0)),
            pl.BlockSpec((None, 3, CONV_W), lambda i: (l, 0, 0)),
        ],
        out_specs=pl.BlockSpec((TS_MIX, POOL_W + CONV_W), lambda i: (i, 0)),
        out_shape=jax.ShapeDtypeStruct((N_TOK, POOL_W + CONV_W), BF16),
        compiler_params=_params(("parallel",)),
        name=f"mixers_{l}",
    )(pin, cv, W["w_pool"], W["pool_scale"], W["conv_w"])


def _out_kernel(x_ref, actx_ref, adec_ref, pc_ref, wo_ref, g1_ref, o_ref):
    i = pl.program_id(0)
    att_w = N_HEADS * V_DIM

    def finish(att):
        y = jnp.dot(att, wo_ref[:att_w, :], preferred_element_type=F32)
        y = y + jnp.dot(pc_ref[...], wo_ref[att_w:, :], preferred_element_type=F32)
        o_ref[...] = x_ref[...] + g1_ref[...] * y

    @pl.when(i < N_CTX // TM_OUT)
    def _():
        finish(actx_ref[...])

    @pl.when(i >= N_CTX // TM_OUT)
    def _():
        finish(adec_ref[...])


def _out_proj(l, x_all, att_ctx, att_dec, pc, mod4, W):
    tm = TM_OUT
    n_ctx_tiles = N_CTX // tm
    row = functools.partial(_mod_row, tm=tm)
    att_w = N_HEADS * V_DIM
    return pl.pallas_call(
        _out_kernel,
        grid=(N_TOK // tm,),
        in_specs=[
            pl.BlockSpec((tm, D_MODEL), lambda i: (i, 0)),
            pl.BlockSpec((tm, att_w), lambda i: (jnp.minimum(i, n_ctx_tiles - 1), 0)),
            pl.BlockSpec((tm, att_w), lambda i: (jnp.maximum(i - n_ctx_tiles, 0), 0)),
            pl.BlockSpec((tm, POOL_W + CONV_W), lambda i: (i, 0)),
            pl.BlockSpec((None, D_MODEL, D_MODEL), lambda i: (l, 0, 0), pipeline_mode=pl.Buffered(1)),
            pl.BlockSpec((None, None, 1, D_MODEL), lambda i: (l, row(i), 0, 2)),
        ],
        out_specs=pl.BlockSpec((tm, D_MODEL), lambda i: (i, 0)),
        out_shape=jax.ShapeDtypeStruct((N_TOK, D_MODEL), F32),
        compiler_params=_params(("parallel",)),
        name=f"out_proj_{l}",
    )(x_all, att_ctx, att_dec, pc, W["w_out"], mod4)


def _ffn_kernel(x_ref, sh_ref, sc_ref, g2_ref, gffn_ref, wg_ref, wu_ref, wd_ref, o_ref, h_scr):
    f = pl.program_id(1)

    @pl.when(f == 0)
    def _():
        x = x_ref[...]
        h = x * _rms(x, D_MODEL) * gffn_ref[...]
        h_scr[...] = (h * (1.0 + sc_ref[...]) + sh_ref[...]).astype(BF16)

    h = h_scr[...]
    gate = jnp.dot(h, wg_ref[...], preferred_element_type=F32)
    up = jnp.dot(h, wu_ref[...], preferred_element_type=F32)
    y = jnp.dot((_silu(gate) * up).astype(BF16), wd_ref[...], preferred_element_type=F32)

    @pl.when(f == 0)
    def _():
        o_ref[...] = y

    @pl.when(f > 0)
    def _():
        o_ref[...] += y

    @pl.when(f == D_FF // TF_FFN - 1)
    def _():
        o_ref[...] = x_ref[...] + g2_ref[...] * o_ref[...]


def _ffn(l, x_all, mod4, W):
    tm, tf = TM_FFN, TF_FFN
    row = functools.partial(_mod_row, tm=tm)
    mod = lambda c: pl.BlockSpec((None, None, 1, D_MODEL), lambda i, f: (l, row(i), 0, c))
    return pl.pallas_call(
        _ffn_kernel,
        grid=(N_TOK // tm, D_FF // tf),
        in_specs=[
            pl.BlockSpec((tm, D_MODEL), lambda i, f: (i, 0)),
            mod(3), mod(4), mod(5),
            pl.BlockSpec((None, 1, D_MODEL), lambda i, f: (l, 0, 0)),
            pl.BlockSpec((None, D_MODEL, tf), lambda i, f: (l, 0, f)),
            pl.BlockSpec((None, D_MODEL, tf), lambda i, f: (l, 0, f)),
            pl.BlockSpec((None, tf, D_MODEL), lambda i, f: (l, f, 0)),
        ],
        out_specs=pl.BlockSpec((tm, D_MODEL), lambda i, f: (i, 0)),
        out_shape=jax.ShapeDtypeStruct((N_TOK, D_MODEL), F32),
        scratch_shapes=[pltpu.VMEM((tm, D_MODEL), BF16)],
        compiler_params=_params(("parallel", "arbitrary")),
        name=f"ffn_{l}",
    )(x_all, mod4, mod4, mod4, W["g_ffn"], W["w_gate"], W["w_up"], W["w_down"])


def _prep_weights(g_mix, g_ffn, w_in, g_q_a, w_uq, g_kv_a, w_ukv, g_qn, g_kn, w_pool, pool_scale, conv_w,
                  w_out, w_gate, w_up, w_down):
    cut = Q_LORA + KV_LORA + ROPE_DIM
    w_in_p = jnp.concatenate(
        [w_in[..., :cut], jnp.zeros((DEPTH, D_MODEL, LANE - ROPE_DIM), w_in.dtype), w_in[..., cut:]], axis=-1)
    w_uq_p = jnp.pad(w_uq.reshape(DEPTH, Q_LORA, N_HEADS, QK_DIM), ((0, 0), (0, 0), (0, 0), (0, HEAD_PAD - QK_DIM)))
    head_vec = lambda g: jnp.pad(g, ((0, 0), (0, HEAD_PAD - QK_DIM))).reshape(DEPTH, 1, HEAD_PAD)
    cos, sin = _rope_tables()
    return dict(
        g_mix=g_mix.reshape(DEPTH, 1, D_MODEL), g_ffn=g_ffn.reshape(DEPTH, 1, D_MODEL),
        w_in=w_in_p.astype(BF16), g_q_a=g_q_a.reshape(DEPTH, 1, Q_LORA),
        w_uq=w_uq_p.reshape(DEPTH, Q_LORA, N_HEADS * HEAD_PAD).astype(BF16),
        g_kv_a=g_kv_a.reshape(DEPTH, 1, KV_LORA), w_ukv=w_ukv.astype(BF16),
        g_qn=head_vec(g_qn), g_kn=head_vec(g_kn), w_pool=w_pool.astype(BF16),
        pool_scale=pool_scale.reshape(DEPTH, 1, POOL_W), conv_w=conv_w,
        w_out=w_out.astype(BF16), w_gate=w_gate.astype(BF16), w_up=w_up.astype(BF16), w_down=w_down.astype(BF16),
        cos=cos, sin=sin)


def kernel(x_prompt, x_sample, cache_ckv, cache_krope, c, c_ctx, w_ada, b_ada, g_mix, g_ffn, w_in, g_q_a, w_uq,
           g_kv_a, w_ukv, g_qn, g_kn, w_pool, pool_scale, conv_w, w_out, w_gate, w_up, w_down):
    W = _prep_weights(g_mix, g_ffn, w_in, g_q_a, w_uq, g_kv_a, w_ukv, g_qn, g_kn, w_pool, pool_scale, conv_w,
                      w_out, w_gate, w_up, w_down)
    m_all = jnp.concatenate([c_ctx[None, :], c, jnp.zeros((MOD_ROWS - 1 - DEC_BATCH, D_MODEL), F32)], axis=0)
    mod4 = _ada(m_all, w_ada, b_ada).reshape(DEPTH, MOD_ROWS, 1, N_MOD)
    cache_kr_pad = jnp.pad(cache_krope, ((0, 0), (0, 0), (0, 0), (0, LANE - ROPE_DIM)))
    kc, vc = _cache_kv(cache_ckv, cache_kr_pad, W)

    x_all = jnp.concatenate([x_prompt.reshape(N_CTX, D_MODEL), x_sample.reshape(N_DEC, D_MODEL)], axis=0)
    new_ckv, new_kr = [], []
    for l in range(DEPTH):
        q, k, v, ckvn, kr, pin, cv = _in_proj(l, x_all, mod4, W)
        att_ctx = _attn_ctx(l, q, k, v)
        att_dec = _attn_dec(l, q, k, v, kc, vc)
        pc = _mixers(l, pin, cv, W)
        x_all = _out_proj(l, x_all, att_ctx, att_dec, pc, mod4, W)
        x_all = _ffn(l, x_all, mod4, W)
        new_ckv.append(ckvn[:N_CTX].reshape(BATCH, SEQ, KV_LORA))
        new_kr.append(kr[:N_CTX].reshape(BATCH, SEQ, ROPE_DIM))
    y_prompt = x_all[:N_CTX].reshape(BATCH, SEQ, D_MODEL)
    y_sample = x_all[N_CTX:].reshape(DEC_BATCH, DEC_SEQ, D_MODEL)
    return y_prompt, y_sample, jnp.stack(new_ckv, axis=1), jnp.stack(new_kr, axis=1)
```

```python
import functools

import jax
import jax.numpy as jnp
from jax import lax
from jax.experimental import pallas as pl
from jax.experimental.pallas import tpu as pltpu

D_MODEL = 2048
BATCH = 16
SEQ = 256
DEPTH = 4
DEC_BATCH = 4
DEC_SEQ = 2048
PAST_LEN = 512
GRID_W = 64
N_HEADS = 8
QK_NOPE = 128
ROPE_DIM = 64
QK_DIM = QK_NOPE + ROPE_DIM
V_DIM = 128
Q_LORA = 512
KV_LORA = 256
POOL_WINDOWS = (2, 4, 8, 16)
POOL_GC = 128
POOL_W = 512
CONV_W = 512
D_FF = 5632
ROPE_BASE = 10000.0
EPS = 1e-6

N_CTX = BATCH * SEQ
N_DEC = DEC_BATCH * DEC_SEQ
N_TOK = N_CTX + N_DEC
N_MOD = 6 * D_MODEL
MOD_ROWS = 8

LANE = 128
HEAD_PAD = 2 * LANE
IN_CQ = 0
IN_CKV = IN_CQ + Q_LORA
IN_KR = IN_CKV + KV_LORA
IN_PIN = IN_KR + LANE
IN_CONV = IN_PIN + POOL_W
IN_PAD = IN_CONV + 3 * CONV_W

TM_IN = 512
SUB_IN = 256
TM_OUT = 512
TM_FFN = 512
TF_FFN = 512
TN_ADA = 1024
TS_MIX = 2048
TQ_DEC = 256
NQ_DEC = 8
VMEM_LIMIT = 56 * 1024 * 1024

F32 = jnp.float32
BF16 = jnp.bfloat16
Q_SCALE = QK_DIM ** -0.5 * 1.4426950408889634


def _mod_row(i, tm):
    n_ctx_tiles = N_CTX // tm
    return jnp.where(i < n_ctx_tiles, 0, 1 + (i - n_ctx_tiles) // (DEC_SEQ // tm))


def _params(sem, vmem=VMEM_LIMIT):
    return pltpu.CompilerParams(dimension_semantics=sem, vmem_limit_bytes=vmem)


def _silu(x):
    return x / (1.0 + jnp.exp(-x))


def _rms(x, n):
    return lax.rsqrt(jnp.sum(x * x, axis=-1, keepdims=True) * (1.0 / n) + EPS)


def _ada_kernel(m_ref, w_ref, b_ref, o_ref):
    a = _silu(m_ref[...]).astype(BF16)
    o_ref[...] = jnp.dot(a, w_ref[...].astype(BF16), preferred_element_type=F32) + b_ref[...]


def _ada(m_all, w_ada, b_ada):
    return pl.pallas_call(
        _ada_kernel,
        grid=(DEPTH, N_MOD // TN_ADA),
        in_specs=[
            pl.BlockSpec((MOD_ROWS, D_MODEL), lambda l, j: (0, 0)),
            pl.BlockSpec((None, D_MODEL, TN_ADA), lambda l, j: (l, 0, j)),
            pl.BlockSpec((None, 1, TN_ADA), lambda l, j: (l, 0, j)),
        ],
        out_specs=pl.BlockSpec((None, MOD_ROWS, TN_ADA), lambda l, j: (l, 0, j)),
        out_shape=jax.ShapeDtypeStruct((DEPTH, MOD_ROWS, N_MOD), F32),
        compiler_params=_params(("parallel", "parallel")),
        name="ada_mod",
    )(m_all, w_ada, b_ada.reshape(DEPTH, 1, N_MOD))


def _rope_tables():
    t = jnp.arange(DEC_SEQ)
    r_pos = (t // GRID_W).astype(F32)
    c_pos = (t % GRID_W).astype(F32)
    nf = ROPE_DIM // 4
    inv = ROPE_BASE ** (-jnp.arange(nf, dtype=F32) / nf)
    ang_r = r_pos[:, None] * inv[None, :]
    ang_c = c_pos[:, None] * inv[None, :]
    zeros = jnp.zeros((DEC_SEQ, LANE - ROPE_DIM), F32)
    cos = jnp.concatenate([jnp.cos(ang_r), jnp.cos(ang_r), jnp.cos(ang_c), jnp.cos(ang_c), zeros], axis=1)
    sin = jnp.concatenate([-jnp.sin(ang_r), jnp.sin(ang_r), -jnp.sin(ang_c), jnp.sin(ang_c), zeros], axis=1)
    ident_cos = jnp.concatenate([jnp.ones((TM_IN, ROPE_DIM), F32), jnp.zeros((TM_IN, LANE - ROPE_DIM), F32)], axis=1)
    ident_sin = jnp.zeros((TM_IN, LANE), F32)
    return jnp.concatenate([ident_cos, cos], axis=0), jnp.concatenate([ident_sin, sin], axis=0)


def _rope(x, cos, sin, first_half):
    n = x.shape[-1]
    swapped = jnp.where(first_half, pltpu.roll(x, n - ROPE_DIM // 4, 1), pltpu.roll(x, ROPE_DIM // 4, 1))
    return x * cos + swapped * sin


def _in_kernel(x_ref, sh_ref, sc_ref, gmix_ref, win_ref, gqa_ref, wuq_ref, gqn_ref, gkva_ref, wukv_ref,
               gkn_ref, cos_ref, sin_ref, q_ref, k_ref, v_ref, ckv_ref, kr_ref, pin_ref, cv_ref):
    lane = lax.broadcasted_iota(jnp.int32, (SUB_IN, LANE), 1)
    first_half = (lane & (ROPE_DIM // 4)) == 0
    gqn = gqn_ref[...]
    gkn = gkn_ref[...]

    for r0 in range(0, TM_IN, SUB_IN):
        rows = slice(r0, r0 + SUB_IN)
        x = x_ref[rows, :]
        h = x * _rms(x, D_MODEL) * gmix_ref[...]
        h = h * (1.0 + sc_ref[...]) + sh_ref[...]
        u = jnp.dot(h.astype(BF16), win_ref[...], preferred_element_type=F32)
        cos = cos_ref[rows, :]
        sin = sin_ref[rows, :]

        cq = u[:, IN_CQ:IN_CQ + Q_LORA]
        cqn = cq * _rms(cq, Q_LORA) * gqa_ref[...]
        q = jnp.dot(cqn.astype(BF16), wuq_ref[...], preferred_element_type=F32)
        for hd in range(N_HEADS):
            qh = q[:, hd * HEAD_PAD:(hd + 1) * HEAD_PAD]
            qh = qh * (_rms(qh, QK_DIM) * Q_SCALE) * gqn
            q_ref[rows, hd * HEAD_PAD:hd * HEAD_PAD + LANE] = qh[:, :LANE].astype(BF16)
            q_ref[rows, hd * HEAD_PAD + LANE:(hd + 1) * HEAD_PAD] = (
                _rope(qh[:, LANE:], cos, sin, first_half).astype(BF16))

        ckv = u[:, IN_CKV:IN_CKV + KV_LORA]
        ckvn = ckv * _rms(ckv, KV_LORA) * gkva_ref[...]
        ckv_ref[rows, :] = ckvn
        kv = jnp.dot(ckvn.astype(BF16), wukv_ref[...], preferred_element_type=F32)
        krb = u[:, IN_KR:IN_KR + LANE]
        kr_ref[rows, :] = krb[:, :ROPE_DIM]
        kr_ss = jnp.sum(krb * krb, axis=-1, keepdims=True)
        krg = _rope(krb * gkn[:, LANE:], cos, sin, first_half)
        for hd in range(N_HEADS):
            kn = kv[:, hd * HEAD_PAD:hd * HEAD_PAD + LANE]
            rs = lax.rsqrt((jnp.sum(kn * kn, axis=-1, keepdims=True) + kr_ss) * (1.0 / QK_DIM) + EPS)
            k_ref[rows, hd * HEAD_PAD:hd * HEAD_PAD + LANE] = (kn * rs * gkn[:, :LANE]).astype(BF16)
            k_ref[rows, hd * HEAD_PAD + LANE:(hd + 1) * HEAD_PAD] = (krg * rs).astype(BF16)
            v_ref[rows, hd * V_DIM:(hd + 1) * V_DIM] = kv[:, hd * HEAD_PAD + LANE:(hd + 1) * HEAD_PAD].astype(BF16)

        pin_ref[rows, :] = u[:, IN_PIN:IN_PIN + POOL_W]
        cv_ref[rows, :] = u[:, IN_CONV:IN_CONV + 3 * CONV_W]


def _in_proj(l, x_all, mod4, W):
    tm = TM_IN
    n_ctx_tiles = N_CTX // tm
    row = functools.partial(_mod_row, tm=tm)
    vec = lambda w: pl.BlockSpec((None, 1, w), lambda i: (l, 0, 0))
    mat = lambda r, c: pl.BlockSpec((None, r, c), lambda i: (l, 0, 0), pipeline_mode=pl.Buffered(1))
    tab = pl.BlockSpec((tm, LANE), lambda i: (jnp.where(i < n_ctx_tiles, 0, 1 + (i - n_ctx_tiles) % (DEC_SEQ // tm)), 0))
    tok = lambda w: pl.BlockSpec((tm, w), lambda i: (i, 0))
    return pl.pallas_call(
        _in_kernel,
        grid=(N_TOK // tm,),
        in_specs=[
            tok(D_MODEL),
            pl.BlockSpec((None, None, 1, D_MODEL), lambda i: (l, row(i), 0, 0)),
            pl.BlockSpec((None, None, 1, D_MODEL), lambda i: (l, row(i), 0, 1)),
            vec(D_MODEL), mat(D_MODEL, IN_PAD), vec(Q_LORA), mat(Q_LORA, N_HEADS * HEAD_PAD), vec(HEAD_PAD),
            vec(KV_LORA), mat(KV_LORA, N_HEADS * HEAD_PAD), vec(HEAD_PAD), tab, tab,
        ],
        out_specs=[tok(N_HEADS * HEAD_PAD), tok(N_HEADS * HEAD_PAD), tok(N_HEADS * V_DIM), tok(KV_LORA),
                   tok(ROPE_DIM), tok(POOL_W), tok(3 * CONV_W)],
        out_shape=[
            jax.ShapeDtypeStruct((N_TOK, N_HEADS * HEAD_PAD), BF16),
            jax.ShapeDtypeStruct((N_TOK, N_HEADS * HEAD_PAD), BF16),
            jax.ShapeDtypeStruct((N_TOK, N_HEADS * V_DIM), BF16),
            jax.ShapeDtypeStruct((N_TOK, KV_LORA), F32),
            jax.ShapeDtypeStruct((N_TOK, ROPE_DIM), F32),
            jax.ShapeDtypeStruct((N_TOK, POOL_W), F32),
            jax.ShapeDtypeStruct((N_TOK, 3 * CONV_W), F32),
        ],
        compiler_params=_params(("parallel",)),
        name=f"in_proj_{l}",
    )(x_all, mod4, mod4, W["g_mix"], W["w_in"], W["g_q_a"], W["w_uq"], W["g_qn"], W["g_kv_a"], W["w_ukv"],
      W["g_kn"], W["cos"], W["sin"])


def _kvc_kernel(ckv_ref, kr_ref, wukv_ref, gkn_ref, k_ref, v_ref):
    kv = jnp.dot(ckv_ref[...].astype(BF16), wukv_ref[...], preferred_element_type=F32)
    krb = kr_ref[...]
    kr_ss = jnp.sum(krb * krb, axis=-1, keepdims=True)
    gkn = gkn_ref[...]
    krg = krb * gkn[:, LANE:]
    for hd in range(N_HEADS):
        kn = kv[:, hd * HEAD_PAD:hd * HEAD_PAD + LANE]
        rs = lax.rsqrt((jnp.sum(kn * kn, axis=-1, keepdims=True) + kr_ss) * (1.0 / QK_DIM) + EPS)
        k_ref[:, hd * HEAD_PAD:hd * HEAD_PAD + LANE] = (kn * rs * gkn[:, :LANE]).astype(BF16)
        k_ref[:, hd * HEAD_PAD + LANE:(hd + 1) * HEAD_PAD] = (krg * rs).astype(BF16)
        v_ref[:, hd * V_DIM:(hd + 1) * V_DIM] = kv[:, hd * HEAD_PAD + LANE:(hd + 1) * HEAD_PAD].astype(BF16)


def _cache_kv(cache_ckv, cache_kr_pad, W):
    return pl.pallas_call(
        _kvc_kernel,
        grid=(DEPTH, DEC_BATCH),
        in_specs=[
            pl.BlockSpec((None, None, PAST_LEN, KV_LORA), lambda l, b: (b, l, 0, 0)),
            pl.BlockSpec((None, None, PAST_LEN, LANE), lambda l, b: (b, l, 0, 0)),
            pl.BlockSpec((None, KV_LORA, N_HEADS * HEAD_PAD), lambda l, b: (l, 0, 0)),
            pl.BlockSpec((None, 1, HEAD_PAD), lambda l, b: (l, 0, 0)),
        ],
        out_specs=[
            pl.BlockSpec((None, None, PAST_LEN, N_HEADS * HEAD_PAD), lambda l, b: (l, b, 0, 0)),
            pl.BlockSpec((None, None, PAST_LEN, N_HEADS * V_DIM), lambda l, b: (l, b, 0, 0)),
        ],
        out_shape=[
            jax.ShapeDtypeStruct((DEPTH, DEC_BATCH, PAST_LEN, N_HEADS * HEAD_PAD), BF16),
            jax.ShapeDtypeStruct((DEPTH, DEC_BATCH, PAST_LEN, N_HEADS * V_DIM), BF16),
        ],
        compiler_params=_params(("parallel", "parallel")),
        name="cache_kv",
    )(cache_ckv, cache_kr_pad, W["w_ukv"], W["g_kn"])


def _softmax_pv(q, ks, vs):
    ss = [lax.dot_general(q, k, (((1,), (1,)), ((), ())), preferred_element_type=F32) for k in ks]
    m = functools.reduce(jnp.maximum, [jnp.max(s, axis=-1, keepdims=True) for s in ss])
    ps = [jnp.exp2(s - m) for s in ss]
    den = functools.reduce(jnp.add, [jnp.sum(p, axis=-1, keepdims=True) for p in ps])
    o = functools.reduce(jnp.add, [jnp.dot(p.astype(BF16), v, preferred_element_type=F32) for p, v in zip(ps, vs)])
    return o / den


def _attn_ctx_kernel(q_ref, k_ref, v_ref, o_ref):
    for hd in range(N_HEADS):
        q = q_ref[:, hd * HEAD_PAD:(hd + 1) * HEAD_PAD]
        k = k_ref[:, hd * HEAD_PAD:(hd + 1) * HEAD_PAD]
        v = v_ref[:, hd * V_DIM:(hd + 1) * V_DIM]
        o_ref[:, hd * V_DIM:(hd + 1) * V_DIM] = _softmax_pv(q, [k], [v]).astype(BF16)


def _attn_ctx(l, q, k, v):
    return pl.pallas_call(
        _attn_ctx_kernel,
        grid=(BATCH,),
        in_specs=[
            pl.BlockSpec((SEQ, N_HEADS * HEAD_PAD), lambda b: (b, 0)),
            pl.BlockSpec((SEQ, N_HEADS * HEAD_PAD), lambda b: (b, 0)),
            pl.BlockSpec((SEQ, N_HEADS * V_DIM), lambda b: (b, 0)),
        ],
        out_specs=pl.BlockSpec((SEQ, N_HEADS * V_DIM), lambda b: (b, 0)),
        out_shape=jax.ShapeDtypeStruct((N_CTX, N_HEADS * V_DIM), BF16),
        compiler_params=_params(("parallel",)),
        name=f"attn_ctx_{l}",
    )(q, k, v)


def _attn_dec_kernel(q_ref, k_ref, v_ref, kc_ref, vc_ref, o_ref):
    k = k_ref[...]
    v = v_ref[...]
    kc = kc_ref[...]
    vc = vc_ref[...]

    def body(i, carry):
        for j in range(NQ_DEC):
            r = pl.multiple_of((i * NQ_DEC + j) * TQ_DEC, TQ_DEC)
            q = q_ref[pl.ds(r, TQ_DEC), :]
            o_ref[pl.ds(r, TQ_DEC), :] = _softmax_pv(q, [k, kc], [v, vc]).astype(BF16)
        return carry

    lax.fori_loop(0, DEC_SEQ // (TQ_DEC * NQ_DEC), body, 0)


def _attn_dec(l, q, k, v, kc, vc):
    first = N_CTX // DEC_SEQ
    return pl.pallas_call(
        _attn_dec_kernel,
        grid=(DEC_BATCH, N_HEADS),
        in_specs=[
            pl.BlockSpec((DEC_SEQ, HEAD_PAD), lambda b, h: (first + b, h)),
            pl.BlockSpec((DEC_SEQ, HEAD_PAD), lambda b, h: (first + b, h)),
            pl.BlockSpec((DEC_SEQ, V_DIM), lambda b, h: (first + b, h)),
            pl.BlockSpec((None, None, PAST_LEN, HEAD_PAD), lambda b, h: (l, b, 0, h)),
            pl.BlockSpec((None, None, PAST_LEN, V_DIM), lambda b, h: (l, b, 0, h)),
        ],
        out_specs=pl.BlockSpec((DEC_SEQ, V_DIM), lambda b, h: (b, h)),
        out_shape=jax.ShapeDtypeStruct((N_DEC, N_HEADS * V_DIM), BF16),
        compiler_params=_params(("parallel", "parallel")),
        name=f"attn_dec_{l}",
    )(q, k, v, kc, vc)


def _mix_kernel(pin_ref, cv_ref, wp_ref, ps_ref, cw_ref, o_ref):
    i = pl.program_id(0)
    seq = jnp.where(i < N_CTX // TS_MIX, SEQ, DEC_SEQ)
    pos = lax.broadcasted_iota(jnp.int32, (TS_MIX, LANE), 0) & (seq - 1)

    def prev(x, d):
        return jnp.where(pos >= d, pltpu.roll(x, d, 0), 0.0)

    def nxt(x, d):
        return jnp.where(pos + d < seq, pltpu.roll(x, TS_MIX - d, 0), 0.0)

    for g, w in enumerate(POOL_WINDOWS):
        half = w // 2
        u = pin_ref[:, g * POOL_GC:(g + 1) * POOL_GC]
        fwd = u
        bwd = u
        span = 1
        while span < half:
            fwd = fwd + nxt(fwd, span)
            bwd = bwd + prev(bwd, span)
            span *= 2
        total = fwd + prev(bwd, 1)
        cnt = (jnp.minimum(pos + half, seq) - jnp.maximum(pos - half, 0)).astype(F32)
        d = total / cnt - u
        y = jnp.dot(d.astype(BF16), wp_ref[g], preferred_element_type=F32)
        o_ref[:, g * POOL_GC:(g + 1) * POOL_GC] = (y * ps_ref[:, g * POOL_GC:(g + 1) * POOL_GC]).astype(BF16)

    for g in range(CONV_W // LANE):
        sl = slice(g * LANE, (g + 1) * LANE)
        ch = cv_ref[:, g * LANE:(g + 1) * LANE]
        cb = cv_ref[:, CONV_W + g * LANE:CONV_W + (g + 1) * LANE]
        cc = cv_ref[:, 2 * CONV_W + g * LANE:2 * CONV_W + (g + 1) * LANE]
        z = cc * ch
        conv = prev(z, 1) * cw_ref[0:1, sl] + z * cw_ref[1:2, sl] + nxt(z, 1) * cw_ref[2:3, sl]
        o_ref[:, POOL_W + g * LANE:POOL_W + (g + 1) * LANE] = (cb * conv).astype(BF16)


def _mixers(l, pin, cv, W):
    return pl.pallas_call(
        _mix_kernel,
        grid=(N_TOK // TS_MIX,),
        in_specs=[
            pl.BlockSpec((TS_MIX, POOL_W), lambda i: (i, 0)),
            pl.BlockSpec((TS_MIX, 3 * CONV_W), lambda i: (i, 0)),
            pl.BlockSpec((None, len(POOL_WINDOWS), POOL_GC, POOL_GC), lambda i: (l, 0, 0, 0)),
            pl.BlockSpec((None, 1, POOL_W), lambda i: (l, 0, 0)),
            pl.BlockSpec((None, 3, CONV_W), lambda i: (l, 0, 0)),
        ],
        out_specs=pl.BlockSpec((TS_MIX, POOL_W + CONV_W), lambda i: (i, 0)),
        out_shape=jax.ShapeDtypeStruct((N_TOK, POOL_W + CONV_W), BF16),
        compiler_params=_params(("parallel",)),
        name=f"mixers_{l}",
    )(pin, cv, W["w_pool"], W["pool_scale"], W["conv_w"])


def _out_kernel(x_ref, actx_ref, adec_ref, pc_ref, wo_ref, g1_ref, o_ref):
    i = pl.program_id(0)
    att_w = N_HEADS * V_DIM

    def finish(att):
        y = jnp.dot(att, wo_ref[:att_w, :], preferred_element_type=F32)
        y = y + jnp.dot(pc_ref[...], wo_ref[att_w:, :], preferred_element_type=F32)
        o_ref[...] = x_ref[...] + g1_ref[...] * y

    @pl.when(i < N_CTX // TM_OUT)
    def _():
        finish(actx_ref[...])

    @pl.when(i >= N_CTX // TM_OUT)
    def _():
        finish(adec_ref[...])


def _out_proj(l, x_all, att_ctx, att_dec, pc, mod4, W):
    tm = TM_OUT
    n_ctx_tiles = N_CTX // tm
    row = functools.partial(_mod_row, tm=tm)
    att_w = N_HEADS * V_DIM
    return pl.pallas_call(
        _out_kernel,
        grid=(N_TOK // tm,),
        in_specs=[
            pl.BlockSpec((tm, D_MODEL), lambda i: (i, 0)),
            pl.BlockSpec((tm, att_w), lambda i: (jnp.minimum(i, n_ctx_tiles - 1), 0)),
            pl.BlockSpec((tm, att_w), lambda i: (jnp.maximum(i - n_ctx_tiles, 0), 0)),
            pl.BlockSpec((tm, POOL_W + CONV_W), lambda i: (i, 0)),
            pl.BlockSpec((None, D_MODEL, D_MODEL), lambda i: (l, 0, 0), pipeline_mode=pl.Buffered(1)),
            pl.BlockSpec((None, None, 1, D_MODEL), lambda i: (l, row(i), 0, 2)),
        ],
        out_specs=pl.BlockSpec((tm, D_MODEL), lambda i: (i, 0)),
        out_shape=jax.ShapeDtypeStruct((N_TOK, D_MODEL), F32),
        compiler_params=_params(("parallel",)),
        name=f"out_proj_{l}",
    )(x_all, att_ctx, att_dec, pc, W["w_out"], mod4)


def _ffn_kernel(x_ref, sh_ref, sc_ref, g2_ref, gffn_ref, wg_ref, wu_ref, wd_ref, o_ref, h_scr):
    f = pl.program_id(1)

    @pl.when(f == 0)
    def _():
        x = x_ref[...]
        h = x * _rms(x, D_MODEL) * gffn_ref[...]
        h_scr[...] = (h * (1.0 + sc_ref[...]) + sh_ref[...]).astype(BF16)
        o_ref[...] = jnp.zeros_like(o_ref)

    h = h_scr[...]
    gate = jnp.dot(h, wg_ref[...], preferred_element_type=F32)
    up = jnp.dot(h, wu_ref[...], preferred_element_type=F32)
    o_ref[...] += jnp.dot((_silu(gate) * up).astype(BF16), wd_ref[...], preferred_element_type=F32)

    @pl.when(f == D_FF // TF_FFN - 1)
    def _():
        o_ref[...] = x_ref[...] + g2_ref[...] * o_ref[...]


def _ffn(l, x_all, mod4, W):
    tm, tf = TM_FFN, TF_FFN
    row = functools.partial(_mod_row, tm=tm)
    mod = lambda c: pl.BlockSpec((None, None, 1, D_MODEL), lambda i, f: (l, row(i), 0, c))
    return pl.pallas_call(
        _ffn_kernel,
        grid=(N_TOK // tm, D_FF // tf),
        in_specs=[
            pl.BlockSpec((tm, D_MODEL), lambda i, f: (i, 0)),
            mod(3), mod(4), mod(5),
            pl.BlockSpec((None, 1, D_MODEL), lambda i, f: (l, 0, 0)),
            pl.BlockSpec((None, D_MODEL, tf), lambda i, f: (l, 0, f)),
            pl.BlockSpec((None, D_MODEL, tf), lambda i, f: (l, 0, f)),
            pl.BlockSpec((None, tf, D_MODEL), lambda i, f: (l, f, 0)),
        ],
        out_specs=pl.BlockSpec((tm, D_MODEL), lambda i, f: (i, 0)),
        out_shape=jax.ShapeDtypeStruct((N_TOK, D_MODEL), F32),
        scratch_shapes=[pltpu.VMEM((tm, D_MODEL), BF16)],
        compiler_params=_params(("parallel", "arbitrary")),
        name=f"ffn_{l}",
    )(x_all, mod4, mod4, mod4, W["g_ffn"], W["w_gate"], W["w_up"], W["w_down"])


def _prep_weights(g_mix, g_ffn, w_in, g_q_a, w_uq, g_kv_a, w_ukv, g_qn, g_kn, w_pool, pool_scale, conv_w,
                  w_out, w_gate, w_up, w_down):
    cut = Q_LORA + KV_LORA + ROPE_DIM
    w_in_p = jnp.concatenate(
        [w_in[..., :cut], jnp.zeros((DEPTH, D_MODEL, LANE - ROPE_DIM), w_in.dtype), w_in[..., cut:]], axis=-1)
    w_uq_p = jnp.pad(w_uq.reshape(DEPTH, Q_LORA, N_HEADS, QK_DIM), ((0, 0), (0, 0), (0, 0), (0, HEAD_PAD - QK_DIM)))
    head_vec = lambda g: jnp.pad(g, ((0, 0), (0, HEAD_PAD - QK_DIM))).reshape(DEPTH, 1, HEAD_PAD)
    cos, sin = _rope_tables()
    return dict(
        g_mix=g_mix.reshape(DEPTH, 1, D_MODEL), g_ffn=g_ffn.reshape(DEPTH, 1, D_MODEL),
        w_in=w_in_p.astype(BF16), g_q_a=g_q_a.reshape(DEPTH, 1, Q_LORA),
        w_uq=w_uq_p.reshape(DEPTH, Q_LORA, N_HEADS * HEAD_PAD).astype(BF16),
        g_kv_a=g_kv_a.reshape(DEPTH, 1, KV_LORA), w_ukv=w_ukv.astype(BF16),
        g_qn=head_vec(g_qn), g_kn=head_vec(g_kn), w_pool=w_pool.astype(BF16),
        pool_scale=pool_scale.reshape(DEPTH, 1, POOL_W), conv_w=conv_w,
        w_out=w_out.astype(BF16), w_gate=w_gate.astype(BF16), w_up=w_up.astype(BF16), w_down=w_down.astype(BF16),
        cos=cos, sin=sin)


def kernel(x_prompt, x_sample, cache_ckv, cache_krope, c, c_ctx, w_ada, b_ada, g_mix, g_ffn, w_in, g_q_a, w_uq,
           g_kv_a, w_ukv, g_qn, g_kn, w_pool, pool_scale, conv_w, w_out, w_gate, w_up, w_down):
    W = _prep_weights(g_mix, g_ffn, w_in, g_q_a, w_uq, g_kv_a, w_ukv, g_qn, g_kn, w_pool, pool_scale, conv_w,
                      w_out, w_gate, w_up, w_down)
    m_all = jnp.concatenate([c_ctx[None, :], c, jnp.zeros((MOD_ROWS - 1 - DEC_BATCH, D_MODEL), F32)], axis=0)
    mod4 = _ada(m_all, w_ada, b_ada).reshape(DEPTH, MOD_ROWS, 1, N_MOD)
    cache_kr_pad = jnp.pad(cache_krope, ((0, 0), (0, 0), (0, 0), (0, LANE - ROPE_DIM)))
    kc, vc = _cache_kv(cache_ckv, cache_kr_pad, W)

    x_all = jnp.concatenate([x_prompt.reshape(N_CTX, D_MODEL), x_sample.reshape(N_DEC, D_MODEL)], axis=0)
    new_ckv, new_kr = [], []
    for l in range(DEPTH):
        q, k, v, ckvn, kr, pin, cv = _in_proj(l, x_all, mod4, W)
        att_ctx = _attn_ctx(l, q, k, v)
        att_dec = _attn_dec(l, q, k, v, kc, vc)
        pc = _mixers(l, pin, cv, W)
        x_all = _out_proj(l, x_all, att_ctx, att_dec, pc, mod4, W)
        x_all = _ffn(l, x_all, mod4, W)
        new_ckv.append(ckvn[:N_CTX].reshape(BATCH, SEQ, KV_LORA))
        new_kr.append(kr[:N_CTX].reshape(BATCH, SEQ, ROPE_DIM))
    y_prompt = x_all[:N_CTX].reshape(BATCH, SEQ, D_MODEL)
    y_sample = x_all[N_CTX:].reshape(DEC_BATCH, DEC_SEQ, D_MODEL)
    return y_prompt, y_sample, jnp.stack(new_ckv, axis=1), jnp.stack(new_kr, axis=1)
```

```python
import functools

import jax
import jax.numpy as jnp
from jax import lax
from jax.experimental import pallas as pl
from jax.experimental.pallas import tpu as pltpu

D_MODEL = 2048
BATCH = 16
SEQ = 256
DEPTH = 4
DEC_BATCH = 4
DEC_SEQ = 2048
PAST_LEN = 512
GRID_W = 64
N_HEADS = 8
QK_NOPE = 128
ROPE_DIM = 64
QK_DIM = QK_NOPE + ROPE_DIM
V_DIM = 128
Q_LORA = 512
KV_LORA = 256
POOL_WINDOWS = (2, 4, 8, 16)
POOL_GC = 128
POOL_W = 512
CONV_W = 512
D_FF = 5632
ROPE_BASE = 10000.0
EPS = 1e-6

N_CTX = BATCH * SEQ
N_DEC = DEC_BATCH * DEC_SEQ
N_TOK = N_CTX + N_DEC
N_MOD = 6 * D_MODEL
MOD_ROWS = 8

LANE = 128
HEAD_PAD = 2 * LANE
IN_CQ = 0
IN_CKV = IN_CQ + Q_LORA
IN_KR = IN_CKV + KV_LORA
IN_PIN = IN_KR + LANE
IN_CONV = IN_PIN + POOL_W
IN_PAD = IN_CONV + 3 * CONV_W

TM_IN = 512
SUB_IN = 256
TM_OUT = 512
SUB_OUT = 256
TM_FFN = 512
TF_FFN = 512
SUB_FFN = 256
TN_ADA = 1024
TS_MIX = 2048
TQ_DEC = 256
VMEM_LIMIT = 56 * 1024 * 1024

F32 = jnp.float32
BF16 = jnp.bfloat16
Q_SCALE = QK_DIM ** -0.5 * 1.4426950408889634


def _mod_row(i, tm):
    n_ctx_tiles = N_CTX // tm
    return jnp.where(i < n_ctx_tiles, 0, 1 + (i - n_ctx_tiles) // (DEC_SEQ // tm))


def _params(sem, vmem=VMEM_LIMIT):
    return pltpu.CompilerParams(dimension_semantics=sem, vmem_limit_bytes=vmem)


def _silu(x):
    return x / (1.0 + jnp.exp(-x))


def _rms(x, n):
    return lax.rsqrt(jnp.sum(x * x, axis=-1, keepdims=True) * (1.0 / n) + EPS)


def _ada_kernel(m_ref, w_ref, b_ref, o_ref):
    a = _silu(m_ref[...]).astype(BF16)
    o_ref[...] = jnp.dot(a, w_ref[...].astype(BF16), preferred_element_type=F32) + b_ref[...]


def _ada(m_all, w_ada, b_ada):
    return pl.pallas_call(
        _ada_kernel,
        grid=(DEPTH, N_MOD // TN_ADA),
        in_specs=[
            pl.BlockSpec((MOD_ROWS, D_MODEL), lambda l, j: (0, 0)),
            pl.BlockSpec((None, D_MODEL, TN_ADA), lambda l, j: (l, 0, j)),
            pl.BlockSpec((None, 1, TN_ADA), lambda l, j: (l, 0, j)),
        ],
        out_specs=pl.BlockSpec((None, MOD_ROWS, TN_ADA), lambda l, j: (l, 0, j)),
        out_shape=jax.ShapeDtypeStruct((DEPTH, MOD_ROWS, N_MOD), F32),
        compiler_params=_params(("parallel", "parallel")),
        name="ada_mod",
    )(m_all, w_ada, b_ada.reshape(DEPTH, 1, N_MOD))


def _rope_tables():
    t = jnp.arange(DEC_SEQ)
    r_pos = (t // GRID_W).astype(F32)
    c_pos = (t % GRID_W).astype(F32)
    nf = ROPE_DIM // 4
    inv = ROPE_BASE ** (-jnp.arange(nf, dtype=F32) / nf)
    ang_r = r_pos[:, None] * inv[None, :]
    ang_c = c_pos[:, None] * inv[None, :]
    zeros = jnp.zeros((DEC_SEQ, LANE - ROPE_DIM), F32)
    cos = jnp.concatenate([jnp.cos(ang_r), jnp.cos(ang_r), jnp.cos(ang_c), jnp.cos(ang_c), zeros], axis=1)
    sin = jnp.concatenate([-jnp.sin(ang_r), jnp.sin(ang_r), -jnp.sin(ang_c), jnp.sin(ang_c), zeros], axis=1)
    ident_cos = jnp.concatenate([jnp.ones((TM_IN, ROPE_DIM), F32), jnp.zeros((TM_IN, LANE - ROPE_DIM), F32)], axis=1)
    ident_sin = jnp.zeros((TM_IN, LANE), F32)
    return jnp.concatenate([ident_cos, cos], axis=0), jnp.concatenate([ident_sin, sin], axis=0)


def _rope(x, cos, sin, first_half):
    n = x.shape[-1]
    swapped = jnp.where(first_half, pltpu.roll(x, n - ROPE_DIM // 4, 1), pltpu.roll(x, ROPE_DIM // 4, 1))
    return x * cos + swapped * sin


def _in_kernel(x_ref, sh_ref, sc_ref, gmix_ref, win_ref, gqa_ref, wuq_ref, gqn_ref, gkva_ref, wukv_ref,
               gkn_ref, cos_ref, sin_ref, q_ref, k_ref, v_ref, ckv_ref, kr_ref, pin_ref, cv_ref):
    lane = lax.broadcasted_iota(jnp.int32, (SUB_IN, LANE), 1)
    first_half = (lane & (ROPE_DIM // 4)) == 0
    gqn = gqn_ref[...]
    gkn = gkn_ref[...]

    for r0 in range(0, TM_IN, SUB_IN):
        rows = slice(r0, r0 + SUB_IN)
        x = x_ref[rows, :]
        h = x * _rms(x, D_MODEL) * gmix_ref[...]
        h = h * (1.0 + sc_ref[...]) + sh_ref[...]
        u = jnp.dot(h.astype(BF16), win_ref[...], preferred_element_type=F32)
        cos = cos_ref[rows, :]
        sin = sin_ref[rows, :]

        cq = u[:, IN_CQ:IN_CQ + Q_LORA]
        cqn = cq * _rms(cq, Q_LORA) * gqa_ref[...]
        q = jnp.dot(cqn.astype(BF16), wuq_ref[...], preferred_element_type=F32)
        for hd in range(N_HEADS):
            qh = q[:, hd * HEAD_PAD:(hd + 1) * HEAD_PAD]
            qh = qh * (_rms(qh, QK_DIM) * Q_SCALE) * gqn
            q_ref[rows, hd * HEAD_PAD:hd * HEAD_PAD + LANE] = qh[:, :LANE].astype(BF16)
            q_ref[rows, hd * HEAD_PAD + LANE:(hd + 1) * HEAD_PAD] = (
                _rope(qh[:, LANE:], cos, sin, first_half).astype(BF16))

        ckv = u[:, IN_CKV:IN_CKV + KV_LORA]
        ckvn = ckv * _rms(ckv, KV_LORA) * gkva_ref[...]
        ckv_ref[rows, :] = ckvn
        kv = jnp.dot(ckvn.astype(BF16), wukv_ref[...], preferred_element_type=F32)
        krb = u[:, IN_KR:IN_KR + LANE]
        kr_ref[rows, :] = krb[:, :ROPE_DIM]
        kr_ss = jnp.sum(krb * krb, axis=-1, keepdims=True)
        krg = _rope(krb * gkn[:, LANE:], cos, sin, first_half)
        for hd in range(N_HEADS):
            kn = kv[:, hd * HEAD_PAD:hd * HEAD_PAD + LANE]
            rs = lax.rsqrt((jnp.sum(kn * kn, axis=-1, keepdims=True) + kr_ss) * (1.0 / QK_DIM) + EPS)
            k_ref[rows, hd * HEAD_PAD:hd * HEAD_PAD + LANE] = (kn * rs * gkn[:, :LANE]).astype(BF16)
            k_ref[rows, hd * HEAD_PAD + LANE:(hd + 1) * HEAD_PAD] = (krg * rs).astype(BF16)
            v_ref[rows, hd * V_DIM:(hd + 1) * V_DIM] = kv[:, hd * HEAD_PAD + LANE:(hd + 1) * HEAD_PAD].astype(BF16)

        pin_ref[rows, :] = u[:, IN_PIN:IN_PIN + POOL_W]
        cv_ref[rows, :] = u[:, IN_CONV:IN_CONV + 3 * CONV_W]


def _in_proj(l, x_all, mod4, W):
    tm = TM_IN
    n_ctx_tiles = N_CTX // tm
    row = functools.partial(_mod_row, tm=tm)
    vec = lambda w: pl.BlockSpec((None, 1, w), lambda i: (l, 0, 0))
    mat = lambda r, c: pl.BlockSpec((None, r, c), lambda i: (l, 0, 0), pipeline_mode=pl.Buffered(1))
    tab = pl.BlockSpec((tm, LANE), lambda i: (jnp.where(i < n_ctx_tiles, 0, 1 + (i - n_ctx_tiles) % (DEC_SEQ // tm)), 0))
    tok = lambda w: pl.BlockSpec((tm, w), lambda i: (i, 0))
    return pl.pallas_call(
        _in_kernel,
        grid=(N_TOK // tm,),
        in_specs=[
            tok(D_MODEL),
            pl.BlockSpec((None, None, 1, D_MODEL), lambda i: (l, row(i), 0, 0)),
            pl.BlockSpec((None, None, 1, D_MODEL), lambda i: (l, row(i), 0, 1)),
            vec(D_MODEL), mat(D_MODEL, IN_PAD), vec(Q_LORA), mat(Q_LORA, N_HEADS * HEAD_PAD), vec(HEAD_PAD),
            vec(KV_LORA), mat(KV_LORA, N_HEADS * HEAD_PAD), vec(HEAD_PAD), tab, tab,
        ],
        out_specs=[tok(N_HEADS * HEAD_PAD), tok(N_HEADS * HEAD_PAD), tok(N_HEADS * V_DIM), tok(KV_LORA),
                   tok(ROPE_DIM), tok(POOL_W), tok(3 * CONV_W)],
        out_shape=[
            jax.ShapeDtypeStruct((N_TOK, N_HEADS * HEAD_PAD), BF16),
            jax.ShapeDtypeStruct((N_TOK, N_HEADS * HEAD_PAD), BF16),
            jax.ShapeDtypeStruct((N_TOK, N_HEADS * V_DIM), BF16),
            jax.ShapeDtypeStruct((N_TOK, KV_LORA), F32),
            jax.ShapeDtypeStruct((N_TOK, ROPE_DIM), F32),
            jax.ShapeDtypeStruct((N_TOK, POOL_W), F32),
            jax.ShapeDtypeStruct((N_TOK, 3 * CONV_W), F32),
        ],
        compiler_params=_params(("parallel",)),
        name=f"in_proj_{l}",
    )(x_all, mod4, mod4, W["g_mix"], W["w_in"], W["g_q_a"], W["w_uq"], W["g_qn"], W["g_kv_a"], W["w_ukv"],
      W["g_kn"], W["cos"], W["sin"])


def _kvc_kernel(ckv_ref, kr_ref, wukv_ref, gkn_ref, k_ref, v_ref):
    kv = jnp.dot(ckv_ref[...].astype(BF16), wukv_ref[...], preferred_element_type=F32)
    krb = kr_ref[...]
    kr_ss = jnp.sum(krb * krb, axis=-1, keepdims=True)
    gkn = gkn_ref[...]
    krg = krb * gkn[:, LANE:]
    for hd in range(N_HEADS):
        kn = kv[:, hd * HEAD_PAD:hd * HEAD_PAD + LANE]
        rs = lax.rsqrt((jnp.sum(kn * kn, axis=-1, keepdims=True) + kr_ss) * (1.0 / QK_DIM) + EPS)
        k_ref[:, hd * HEAD_PAD:hd * HEAD_PAD + LANE] = (kn * rs * gkn[:, :LANE]).astype(BF16)
        k_ref[:, hd * HEAD_PAD + LANE:(hd + 1) * HEAD_PAD] = (krg * rs).astype(BF16)
        v_ref[:, hd * V_DIM:(hd + 1) * V_DIM] = kv[:, hd * HEAD_PAD + LANE:(hd + 1) * HEAD_PAD].astype(BF16)


def _cache_kv(cache_ckv, cache_kr_pad, W):
    return pl.pallas_call(
        _kvc_kernel,
        grid=(DEPTH, DEC_BATCH),
        in_specs=[
            pl.BlockSpec((None, None, PAST_LEN, KV_LORA), lambda l, b: (b, l, 0, 0)),
            pl.BlockSpec((None, None, PAST_LEN, LANE), lambda l, b: (b, l, 0, 0)),
            pl.BlockSpec((None, KV_LORA, N_HEADS * HEAD_PAD), lambda l, b: (l, 0, 0)),
            pl.BlockSpec((None, 1, HEAD_PAD), lambda l, b: (l, 0, 0)),
        ],
        out_specs=[
            pl.BlockSpec((None, None, PAST_LEN, N_HEADS * HEAD_PAD), lambda l, b: (l, b, 0, 0)),
            pl.BlockSpec((None, None, PAST_LEN, N_HEADS * V_DIM), lambda l, b: (l, b, 0, 0)),
        ],
        out_shape=[
            jax.ShapeDtypeStruct((DEPTH, DEC_BATCH, PAST_LEN, N_HEADS * HEAD_PAD), BF16),
            jax.ShapeDtypeStruct((DEPTH, DEC_BATCH, PAST_LEN, N_HEADS * V_DIM), BF16),
        ],
        compiler_params=_params(("parallel", "parallel")),
        name="cache_kv",
    )(cache_ckv, cache_kr_pad, W["w_ukv"], W["g_kn"])


def _softmax_pv(q, ks, vs):
    ss = [lax.dot_general(q, k, (((1,), (1,)), ((), ())), preferred_element_type=F32) for k in ks]
    m = functools.reduce(jnp.maximum, [jnp.max(s, axis=-1, keepdims=True) for s in ss])
    ps = [jnp.exp2(s - m) for s in ss]
    den = functools.reduce(jnp.add, [jnp.sum(p, axis=-1, keepdims=True) for p in ps])
    o = functools.reduce(jnp.add, [jnp.dot(p.astype(BF16), v, preferred_element_type=F32) for p, v in zip(ps, vs)])
    return o / den


def _attn_ctx_kernel(q_ref, k_ref, v_ref, o_ref):
    for hd in range(N_HEADS):
        q = q_ref[:, hd * HEAD_PAD:(hd + 1) * HEAD_PAD]
        k = k_ref[:, hd * HEAD_PAD:(hd + 1) * HEAD_PAD]
        v = v_ref[:, hd * V_DIM:(hd + 1) * V_DIM]
        o_ref[:, hd * V_DIM:(hd + 1) * V_DIM] = _softmax_pv(q, [k], [v]).astype(BF16)


def _attn_ctx(l, q, k, v):
    return pl.pallas_call(
        _attn_ctx_kernel,
        grid=(BATCH,),
        in_specs=[
            pl.BlockSpec((SEQ, N_HEADS * HEAD_PAD), lambda b: (b, 0)),
            pl.BlockSpec((SEQ, N_HEADS * HEAD_PAD), lambda b: (b, 0)),
            pl.BlockSpec((SEQ, N_HEADS * V_DIM), lambda b: (b, 0)),
        ],
        out_specs=pl.BlockSpec((SEQ, N_HEADS * V_DIM), lambda b: (b, 0)),
        out_shape=jax.ShapeDtypeStruct((N_CTX, N_HEADS * V_DIM), BF16),
        compiler_params=_params(("parallel",)),
        name=f"attn_ctx_{l}",
    )(q, k, v)


def _attn_dec_kernel(q_ref, k_ref, v_ref, kc_ref, vc_ref, o_ref):
    ks = [k_ref[...], kc_ref[...]]
    vs = [v_ref[...], vc_ref[...]]
    for r in range(0, DEC_SEQ, TQ_DEC):
        o_ref[r:r + TQ_DEC, :] = _softmax_pv(q_ref[r:r + TQ_DEC, :], ks, vs).astype(BF16)


def _attn_dec(l, q, k, v, kc, vc):
    first = N_CTX // DEC_SEQ
    return pl.pallas_call(
        _attn_dec_kernel,
        grid=(DEC_BATCH, N_HEADS),
        in_specs=[
            pl.BlockSpec((DEC_SEQ, HEAD_PAD), lambda b, h: (first + b, h)),
            pl.BlockSpec((DEC_SEQ, HEAD_PAD), lambda b, h: (first + b, h)),
            pl.BlockSpec((DEC_SEQ, V_DIM), lambda b, h: (first + b, h)),
            pl.BlockSpec((None, None, PAST_LEN, HEAD_PAD), lambda b, h: (l, b, 0, h)),
            pl.BlockSpec((None, None, PAST_LEN, V_DIM), lambda b, h: (l, b, 0, h)),
        ],
        out_specs=pl.BlockSpec((DEC_SEQ, V_DIM), lambda b, h: (b, h)),
        out_shape=jax.ShapeDtypeStruct((N_DEC, N_HEADS * V_DIM), BF16),
        compiler_params=_params(("parallel", "parallel")),
        name=f"attn_dec_{l}",
    )(q, k, v, kc, vc)


def _mix_kernel(pin_ref, cv_ref, wp_ref, ps_ref, cw_ref, o_ref):
    i = pl.program_id(0)
    seq = jnp.where(i < N_CTX // TS_MIX, SEQ, DEC_SEQ)
    pos = lax.broadcasted_iota(jnp.int32, (TS_MIX, LANE), 0) & (seq - 1)

    def prev(x, d):
        return jnp.where(pos >= d, pltpu.roll(x, d, 0), 0.0)

    def nxt(x, d):
        return jnp.where(pos + d < seq, pltpu.roll(x, TS_MIX - d, 0), 0.0)

    for g, w in enumerate(POOL_WINDOWS):
        half = w // 2
        u = pin_ref[:, g * POOL_GC:(g + 1) * POOL_GC]
        fwd = u
        bwd = u
        span = 1
        while span < half:
            fwd = fwd + nxt(fwd, span)
            bwd = bwd + prev(bwd, span)
            span *= 2
        total = fwd + prev(bwd, 1)
        cnt = (jnp.minimum(pos + half, seq) - jnp.maximum(pos - half, 0)).astype(F32)
        d = total / cnt - u
        y = jnp.dot(d.astype(BF16), wp_ref[g], preferred_element_type=F32)
        o_ref[:, g * POOL_GC:(g + 1) * POOL_GC] = (y * ps_ref[:, g * POOL_GC:(g + 1) * POOL_GC]).astype(BF16)

    for g in range(CONV_W // LANE):
        sl = slice(g * LANE, (g + 1) * LANE)
        ch = cv_ref[:, g * LANE:(g + 1) * LANE]
        cb = cv_ref[:, CONV_W + g * LANE:CONV_W + (g + 1) * LANE]
        cc = cv_ref[:, 2 * CONV_W + g * LANE:2 * CONV_W + (g + 1) * LANE]
        z = cc * ch
        conv = prev(z, 1) * cw_ref[0:1, sl] + z * cw_ref[1:2, sl] + nxt(z, 1) * cw_ref[2:3, sl]
        o_ref[:, POOL_W + g * LANE:POOL_W + (g + 1) * LANE] = (cb * conv).astype(BF16)


def _mixers(l, pin, cv, W):
    return pl.pallas_call(
        _mix_kernel,
        grid=(N_TOK // TS_MIX,),
        in_specs=[
            pl.BlockSpec((TS_MIX, POOL_W), lambda i: (i, 0)),
            pl.BlockSpec((TS_MIX, 3 * CONV_W), lambda i: (i, 0)),
            pl.BlockSpec((None, len(POOL_WINDOWS), POOL_GC, POOL_GC), lambda i: (l, 0, 0, 0)),
            pl.BlockSpec((None, 1, POOL_W), lambda i: (l, 0, 0)),
            pl.BlockSpec((None, 3, CONV_W), lambda i: (l, 0, 0)),
        ],
        out_specs=pl.BlockSpec((TS_MIX, POOL_W + CONV_W), lambda i: (i, 0)),
        out_shape=jax.ShapeDtypeStruct((N_TOK, POOL_W + CONV_W), BF16),
        compiler_params=_params(("parallel",)),
        name=f"mixers_{l}",
    )(pin, cv, W["w_pool"], W["pool_scale"], W["conv_w"])


def _out_kernel(x_ref, actx_ref, adec_ref, pc_ref, wo_ref, g1_ref, sh_ref, sc_ref, gffn_ref, o_ref, h_ref):
    i = pl.program_id(0)
    att_w = N_HEADS * V_DIM

    def finish(att_ref):
        for r0 in range(0, TM_OUT, SUB_OUT):
            rows = slice(r0, r0 + SUB_OUT)
            y = jnp.dot(att_ref[rows, :], wo_ref[:att_w, :], preferred_element_type=F32)
            y = y + jnp.dot(pc_ref[rows, :], wo_ref[att_w:, :], preferred_element_type=F32)
            x = x_ref[rows, :] + g1_ref[...] * y
            o_ref[rows, :] = x
            h = x * _rms(x, D_MODEL) * gffn_ref[...]
            h_ref[rows, :] = (h * (1.0 + sc_ref[...]) + sh_ref[...]).astype(BF16)

    @pl.when(i < N_CTX // TM_OUT)
    def _():
        finish(actx_ref)

    @pl.when(i >= N_CTX // TM_OUT)
    def _():
        finish(adec_ref)


def _out_proj(l, x_all, att_ctx, att_dec, pc, mod4, W):
    tm = TM_OUT
    n_ctx_tiles = N_CTX // tm
    row = functools.partial(_mod_row, tm=tm)
    att_w = N_HEADS * V_DIM
    return pl.pallas_call(
        _out_kernel,
        grid=(N_TOK // tm,),
        in_specs=[
            pl.BlockSpec((tm, D_MODEL), lambda i: (i, 0)),
            pl.BlockSpec((tm, att_w), lambda i: (jnp.minimum(i, n_ctx_tiles - 1), 0)),
            pl.BlockSpec((tm, att_w), lambda i: (jnp.maximum(i - n_ctx_tiles, 0), 0)),
            pl.BlockSpec((tm, POOL_W + CONV_W), lambda i: (i, 0)),
            pl.BlockSpec((None, D_MODEL, D_MODEL), lambda i: (l, 0, 0), pipeline_mode=pl.Buffered(1)),
            pl.BlockSpec((None, None, 1, D_MODEL), lambda i: (l, row(i), 0, 2)),
            pl.BlockSpec((None, None, 1, D_MODEL), lambda i: (l, row(i), 0, 3)),
            pl.BlockSpec((None, None, 1, D_MODEL), lambda i: (l, row(i), 0, 4)),
            pl.BlockSpec((None, 1, D_MODEL), lambda i: (l, 0, 0)),
        ],
        out_specs=[pl.BlockSpec((tm, D_MODEL), lambda i: (i, 0)), pl.BlockSpec((tm, D_MODEL), lambda i: (i, 0))],
        out_shape=[jax.ShapeDtypeStruct((N_TOK, D_MODEL), F32), jax.ShapeDtypeStruct((N_TOK, D_MODEL), BF16)],
        compiler_params=_params(("parallel",)),
        name=f"out_proj_{l}",
    )(x_all, att_ctx, att_dec, pc, W["w_out"], mod4, mod4, mod4, W["g_ffn"])


def _ffn_kernel(x_ref, h_ref, g2_ref, wg_ref, wu_ref, wd_ref, *out_refs, split):
    f = pl.program_id(1)
    o_ref = out_refs[-1]

    @pl.when(f == 0)
    def _():
        o_ref[...] = x_ref[...]

    h = h_ref[...]
    y = None
    for c0 in range(0, TF_FFN, SUB_FFN):
        gate = jnp.dot(h, wg_ref[:, c0:c0 + SUB_FFN], preferred_element_type=F32)
        up = jnp.dot(h, wu_ref[:, c0:c0 + SUB_FFN], preferred_element_type=F32)
        part = jnp.dot((_silu(gate) * up).astype(BF16), wd_ref[c0:c0 + SUB_FFN, :], preferred_element_type=F32)
        y = part if y is None else y + part
    o_ref[...] += g2_ref[...] * y

    if split:
        octx_ref, odec_ref, _ = out_refs
        last = f == D_FF // TF_FFN - 1
        is_ctx = pl.program_id(0) < N_CTX // TM_FFN

        @pl.when(last & is_ctx)
        def _():
            octx_ref[...] = o_ref[...]

        @pl.when(last & jnp.logical_not(is_ctx))
        def _():
            odec_ref[...] = o_ref[...]


def _ffn(l, x_all, h_all, mod4, W, split=False):
    tm, tf = TM_FFN, TF_FFN
    n_ctx_tiles = N_CTX // tm
    row = functools.partial(_mod_row, tm=tm)
    if split:
        out_specs = [pl.BlockSpec((tm, D_MODEL), lambda i, f: (jnp.minimum(i, n_ctx_tiles - 1), 0)),
                     pl.BlockSpec((tm, D_MODEL), lambda i, f: (jnp.maximum(i - n_ctx_tiles, 0), 0))]
        out_shape = [jax.ShapeDtypeStruct((N_CTX, D_MODEL), F32), jax.ShapeDtypeStruct((N_DEC, D_MODEL), F32)]
        scratch = [pltpu.VMEM((tm, D_MODEL), F32)]
    else:
        out_specs = pl.BlockSpec((tm, D_MODEL), lambda i, f: (i, 0))
        out_shape = jax.ShapeDtypeStruct((N_TOK, D_MODEL), F32)
        scratch = []
    return pl.pallas_call(
        functools.partial(_ffn_kernel, split=split),
        grid=(N_TOK // tm, D_FF // tf),
        in_specs=[
            pl.BlockSpec((tm, D_MODEL), lambda i, f: (i, 0)),
            pl.BlockSpec((tm, D_MODEL), lambda i, f: (i, 0)),
            pl.BlockSpec((None, None, 1, D_MODEL), lambda i, f: (l, row(i), 0, 5)),
            pl.BlockSpec((None, D_MODEL, tf), lambda i, f: (l, 0, f)),
            pl.BlockSpec((None, D_MODEL, tf), lambda i, f: (l, 0, f)),
            pl.BlockSpec((None, tf, D_MODEL), lambda i, f: (l, f, 0)),
        ],
        out_specs=out_specs,
        out_shape=out_shape,
        scratch_shapes=scratch,
        compiler_params=_params(("arbitrary", "arbitrary") if split else ("parallel", "arbitrary")),
        name=f"ffn_{l}",
    )(x_all, h_all, mod4, W["w_gate"], W["w_up"], W["w_down"])


def _prep_weights(g_mix, g_ffn, w_in, g_q_a, w_uq, g_kv_a, w_ukv, g_qn, g_kn, w_pool, pool_scale, conv_w,
                  w_out, w_gate, w_up, w_down):
    cut = Q_LORA + KV_LORA + ROPE_DIM
    w_in_p = jnp.concatenate(
        [w_in[..., :cut], jnp.zeros((DEPTH, D_MODEL, LANE - ROPE_DIM), w_in.dtype), w_in[..., cut:]], axis=-1)
    w_uq_p = jnp.pad(w_uq.reshape(DEPTH, Q_LORA, N_HEADS, QK_DIM), ((0, 0), (0, 0), (0, 0), (0, HEAD_PAD - QK_DIM)))
    head_vec = lambda g: jnp.pad(g, ((0, 0), (0, HEAD_PAD - QK_DIM))).reshape(DEPTH, 1, HEAD_PAD)
    cos, sin = _rope_tables()
    return dict(
        g_mix=g_mix.reshape(DEPTH, 1, D_MODEL), g_ffn=g_ffn.reshape(DEPTH, 1, D_MODEL),
        w_in=w_in_p.astype(BF16), g_q_a=g_q_a.reshape(DEPTH, 1, Q_LORA),
        w_uq=w_uq_p.reshape(DEPTH, Q_LORA, N_HEADS * HEAD_PAD).astype(BF16),
        g_kv_a=g_kv_a.reshape(DEPTH, 1, KV_LORA), w_ukv=w_ukv.astype(BF16),
        g_qn=head_vec(g_qn), g_kn=head_vec(g_kn), w_pool=w_pool.astype(BF16),
        pool_scale=pool_scale.reshape(DEPTH, 1, POOL_W), conv_w=conv_w,
        w_out=w_out.astype(BF16), w_gate=w_gate.astype(BF16), w_up=w_up.astype(BF16), w_down=w_down.astype(BF16),
        cos=cos, sin=sin)


def kernel(x_prompt, x_sample, cache_ckv, cache_krope, c, c_ctx, w_ada, b_ada, g_mix, g_ffn, w_in, g_q_a, w_uq,
           g_kv_a, w_ukv, g_qn, g_kn, w_pool, pool_scale, conv_w, w_out, w_gate, w_up, w_down):
    W = _prep_weights(g_mix, g_ffn, w_in, g_q_a, w_uq, g_kv_a, w_ukv, g_qn, g_kn, w_pool, pool_scale, conv_w,
                      w_out, w_gate, w_up, w_down)
    m_all = jnp.concatenate([c_ctx[None, :], c, jnp.zeros((MOD_ROWS - 1 - DEC_BATCH, D_MODEL), F32)], axis=0)
    mod4 = _ada(m_all, w_ada, b_ada).reshape(DEPTH, MOD_ROWS, 1, N_MOD)
    cache_kr_pad = jnp.pad(cache_krope, ((0, 0), (0, 0), (0, 0), (0, LANE - ROPE_DIM)))
    kc, vc = _cache_kv(cache_ckv, cache_kr_pad, W)

    x_all = jnp.concatenate([x_prompt.reshape(N_CTX, D_MODEL), x_sample.reshape(N_DEC, D_MODEL)], axis=0)
    new_ckv, new_kr = [], []
    for l in range(DEPTH):
        q, k, v, ckvn, kr, pin, cv = _in_proj(l, x_all, mod4, W)
        att_ctx = _attn_ctx(l, q, k, v)
        att_dec = _attn_dec(l, q, k, v, kc, vc)
        pc = _mixers(l, pin, cv, W)
        x_all, h_all = _out_proj(l, x_all, att_ctx, att_dec, pc, mod4, W)
        if l < DEPTH - 1:
            x_all = _ffn(l, x_all, h_all, mod4, W)
        else:
            y_ctx, y_dec = _ffn(l, x_all, h_all, mod4, W, split=True)
        new_ckv.append(ckvn[:N_CTX].reshape(BATCH, SEQ, KV_LORA))
        new_kr.append(kr[:N_CTX].reshape(BATCH, SEQ, ROPE_DIM))
    y_prompt = y_ctx.reshape(BATCH, SEQ, D_MODEL)
    y_sample = y_dec.reshape(DEC_BATCH, DEC_SEQ, D_MODEL)
    return y_prompt, y_sample, jnp.stack(new_ckv, axis=1), jnp.stack(new_kr, axis=1)
```

```python
import functools

import jax
import jax.numpy as jnp
from jax import lax
from jax.experimental import pallas as pl
from jax.experimental.pallas import tpu as pltpu

D_MODEL = 2048
BATCH = 16
SEQ = 256
DEPTH = 4
DEC_BATCH = 4
DEC_SEQ = 2048
PAST_LEN = 512
GRID_W = 64
N_HEADS = 8
QK_NOPE = 128
ROPE_DIM = 64
QK_DIM = QK_NOPE + ROPE_DIM
V_DIM = 128
Q_LORA = 512
KV_LORA = 256
POOL_WINDOWS = (2, 4, 8, 16)
POOL_GC = 128
POOL_W = 512
CONV_W = 512
D_FF = 5632
ROPE_BASE = 10000.0
EPS = 1e-6

N_CTX = BATCH * SEQ
N_DEC = DEC_BATCH * DEC_SEQ
N_TOK = N_CTX + N_DEC
N_MOD = 6 * D_MODEL
MOD_ROWS = 8

LANE = 128
HEAD_PAD = 2 * LANE
IN_CQ = 0
IN_CKV = IN_CQ + Q_LORA
IN_KR = IN_CKV + KV_LORA
IN_A = IN_KR + LANE
IN_A_SRC = IN_KR + ROPE_DIM
IN_B = POOL_W + 3 * CONV_W

TM_IN = 512
SUB_IN = 256
TM_OUT = 512
SUB_OUT = 256
TM_FFN = 1024
TM_FFN_SPLIT = 512
TF_FFN = 512
SUB_FFN = 256
TN_ADA = 1024
TS_MIX = 2048
TQ_DEC = 256
VMEM_LIMIT = 56 * 1024 * 1024
VMEM_LIMIT_FFN = 62 * 1024 * 1024

F32 = jnp.float32
BF16 = jnp.bfloat16
Q_SCALE = QK_DIM ** -0.5 * 1.4426950408889634


def _mod_row(i, tm):
    n_ctx_tiles = N_CTX // tm
    return jnp.where(i < n_ctx_tiles, 0, 1 + (i - n_ctx_tiles) // (DEC_SEQ // tm))


def _params(sem, vmem=VMEM_LIMIT):
    return pltpu.CompilerParams(dimension_semantics=sem, vmem_limit_bytes=vmem)


def _silu(x):
    return x / (1.0 + jnp.exp(-x))


def _rms(x, n):
    return lax.rsqrt(jnp.sum(x * x, axis=-1, keepdims=True) * (1.0 / n) + EPS)


def _ada_kernel(m_ref, w_ref, b_ref, o_ref):
    a = _silu(m_ref[...]).astype(BF16)
    o_ref[...] = jnp.dot(a, w_ref[...].astype(BF16), preferred_element_type=F32) + b_ref[...]


def _ada(m_all, w_ada, b_ada):
    return pl.pallas_call(
        _ada_kernel,
        grid=(DEPTH, N_MOD // TN_ADA),
        in_specs=[
            pl.BlockSpec((MOD_ROWS, D_MODEL), lambda l, j: (0, 0)),
            pl.BlockSpec((None, D_MODEL, TN_ADA), lambda l, j: (l, 0, j)),
            pl.BlockSpec((None, 1, TN_ADA), lambda l, j: (l, 0, j)),
        ],
        out_specs=pl.BlockSpec((None, MOD_ROWS, TN_ADA), lambda l, j: (l, 0, j)),
        out_shape=jax.ShapeDtypeStruct((DEPTH, MOD_ROWS, N_MOD), F32),
        compiler_params=_params(("parallel", "parallel")),
        name="ada_mod",
    )(m_all, w_ada, b_ada.reshape(DEPTH, 1, N_MOD))


def _rope_tables():
    t = jnp.arange(DEC_SEQ)
    r_pos = (t // GRID_W).astype(F32)
    c_pos = (t % GRID_W).astype(F32)
    nf = ROPE_DIM // 4
    inv = ROPE_BASE ** (-jnp.arange(nf, dtype=F32) / nf)
    ang_r = r_pos[:, None] * inv[None, :]
    ang_c = c_pos[:, None] * inv[None, :]
    zeros = jnp.zeros((DEC_SEQ, LANE - ROPE_DIM), F32)
    cos = jnp.concatenate([jnp.cos(ang_r), jnp.cos(ang_r), jnp.cos(ang_c), jnp.cos(ang_c), zeros], axis=1)
    sin = jnp.concatenate([-jnp.sin(ang_r), jnp.sin(ang_r), -jnp.sin(ang_c), jnp.sin(ang_c), zeros], axis=1)
    ident_cos = jnp.concatenate([jnp.ones((TM_IN, ROPE_DIM), F32), jnp.zeros((TM_IN, LANE - ROPE_DIM), F32)], axis=1)
    ident_sin = jnp.zeros((TM_IN, LANE), F32)
    return jnp.concatenate([ident_cos, cos], axis=0), jnp.concatenate([ident_sin, sin], axis=0)


def _rope(x, cos, sin, first_half):
    n = x.shape[-1]
    swapped = jnp.where(first_half, pltpu.roll(x, n - ROPE_DIM // 4, 1), pltpu.roll(x, ROPE_DIM // 4, 1))
    return x * cos + swapped * sin


def _in_kernel(x_ref, sh_ref, sc_ref, gmix_ref, wa_ref, wb_ref, gqa_ref, wuq_ref, gqn_ref, gkva_ref, wukv_ref,
               gkn_ref, cos_ref, sin_ref, q_ref, k_ref, v_ref, ckv_ref, kr_ref, pin_ref, cv_ref):
    lane = lax.broadcasted_iota(jnp.int32, (SUB_IN, LANE), 1)
    first_half = (lane & (ROPE_DIM // 4)) == 0
    gqn = gqn_ref[...]
    gkn = gkn_ref[...]

    for r0 in range(0, TM_IN, SUB_IN):
        rows = slice(r0, r0 + SUB_IN)
        x = x_ref[rows, :]
        h = x * _rms(x, D_MODEL) * gmix_ref[...]
        h = h * (1.0 + sc_ref[...]) + sh_ref[...]
        hb = h.astype(BF16)
        u = jnp.dot(hb, wa_ref[...], preferred_element_type=F32)
        um = jnp.dot(hb, wb_ref[...], preferred_element_type=F32)
        cos = cos_ref[rows, :]
        sin = sin_ref[rows, :]

        cq = u[:, IN_CQ:IN_CQ + Q_LORA]
        cqn = cq * _rms(cq, Q_LORA) * gqa_ref[...]
        q = jnp.dot(cqn.astype(BF16), wuq_ref[...], preferred_element_type=F32)
        for hd in range(N_HEADS):
            qh = q[:, hd * HEAD_PAD:(hd + 1) * HEAD_PAD]
            qh = qh * (_rms(qh, QK_DIM) * Q_SCALE) * gqn
            q_ref[rows, hd * HEAD_PAD:hd * HEAD_PAD + LANE] = qh[:, :LANE].astype(BF16)
            q_ref[rows, hd * HEAD_PAD + LANE:(hd + 1) * HEAD_PAD] = (
                _rope(qh[:, LANE:], cos, sin, first_half).astype(BF16))

        ckv = u[:, IN_CKV:IN_CKV + KV_LORA]
        ckvn = ckv * _rms(ckv, KV_LORA) * gkva_ref[...]
        ckv_ref[rows, :] = ckvn
        kv = jnp.dot(ckvn.astype(BF16), wukv_ref[...], preferred_element_type=F32)
        krb = u[:, IN_KR:IN_KR + LANE]
        kr_ref[rows, :] = krb[:, :ROPE_DIM]
        kr_ss = jnp.sum(krb * krb, axis=-1, keepdims=True)
        krg = _rope(krb * gkn[:, LANE:], cos, sin, first_half)
        for hd in range(N_HEADS):
            kn = kv[:, hd * HEAD_PAD:hd * HEAD_PAD + LANE]
            rs = lax.rsqrt((jnp.sum(kn * kn, axis=-1, keepdims=True) + kr_ss) * (1.0 / QK_DIM) + EPS)
            k_ref[rows, hd * HEAD_PAD:hd * HEAD_PAD + LANE] = (kn * rs * gkn[:, :LANE]).astype(BF16)
            k_ref[rows, hd * HEAD_PAD + LANE:(hd + 1) * HEAD_PAD] = (krg * rs).astype(BF16)
            v_ref[rows, hd * V_DIM:(hd + 1) * V_DIM] = kv[:, hd * HEAD_PAD + LANE:(hd + 1) * HEAD_PAD].astype(BF16)

        pin_ref[rows, :] = um[:, :POOL_W]
        cv_ref[rows, :] = um[:, POOL_W:]


def _in_proj(l, x_all, mod4, W):
    tm = TM_IN
    n_ctx_tiles = N_CTX // tm
    row = functools.partial(_mod_row, tm=tm)
    vec = lambda w: pl.BlockSpec((None, 1, w), lambda i: (l, 0, 0))
    mat = lambda r, c: pl.BlockSpec((None, r, c), lambda i: (l, 0, 0), pipeline_mode=pl.Buffered(1))
    tab = pl.BlockSpec((tm, LANE), lambda i: (jnp.where(i < n_ctx_tiles, 0, 1 + (i - n_ctx_tiles) % (DEC_SEQ // tm)), 0))
    tok = lambda w: pl.BlockSpec((tm, w), lambda i: (i, 0))
    return pl.pallas_call(
        _in_kernel,
        grid=(N_TOK // tm,),
        in_specs=[
            tok(D_MODEL),
            pl.BlockSpec((None, None, 1, D_MODEL), lambda i: (l, row(i), 0, 0)),
            pl.BlockSpec((None, None, 1, D_MODEL), lambda i: (l, row(i), 0, 1)),
            vec(D_MODEL), mat(D_MODEL, IN_A), mat(D_MODEL, IN_B), vec(Q_LORA), mat(Q_LORA, N_HEADS * HEAD_PAD),
            vec(HEAD_PAD),
            vec(KV_LORA), mat(KV_LORA, N_HEADS * HEAD_PAD), vec(HEAD_PAD), tab, tab,
        ],
        out_specs=[tok(N_HEADS * HEAD_PAD), tok(N_HEADS * HEAD_PAD), tok(N_HEADS * V_DIM), tok(KV_LORA),
                   tok(ROPE_DIM), tok(POOL_W), tok(3 * CONV_W)],
        out_shape=[
            jax.ShapeDtypeStruct((N_TOK, N_HEADS * HEAD_PAD), BF16),
            jax.ShapeDtypeStruct((N_TOK, N_HEADS * HEAD_PAD), BF16),
            jax.ShapeDtypeStruct((N_TOK, N_HEADS * V_DIM), BF16),
            jax.ShapeDtypeStruct((N_TOK, KV_LORA), F32),
            jax.ShapeDtypeStruct((N_TOK, ROPE_DIM), F32),
            jax.ShapeDtypeStruct((N_TOK, POOL_W), F32),
            jax.ShapeDtypeStruct((N_TOK, 3 * CONV_W), F32),
        ],
        compiler_params=_params(("parallel",)),
        name=f"in_proj_{l}",
    )(x_all, mod4, mod4, W["g_mix"], W["w_in_a"], W["w_in_b"], W["g_q_a"], W["w_uq"], W["g_qn"], W["g_kv_a"], W["w_ukv"],
      W["g_kn"], W["cos"], W["sin"])


def _kvc_kernel(ckv_ref, kr_ref, wukv_ref, gkn_ref, k_ref, v_ref):
    kv = jnp.dot(ckv_ref[...].astype(BF16), wukv_ref[...], preferred_element_type=F32)
    krb = kr_ref[...]
    kr_ss = jnp.sum(krb * krb, axis=-1, keepdims=True)
    gkn = gkn_ref[...]
    krg = krb * gkn[:, LANE:]
    for hd in range(N_HEADS):
        kn = kv[:, hd * HEAD_PAD:hd * HEAD_PAD + LANE]
        rs = lax.rsqrt((jnp.sum(kn * kn, axis=-1, keepdims=True) + kr_ss) * (1.0 / QK_DIM) + EPS)
        k_ref[:, hd * HEAD_PAD:hd * HEAD_PAD + LANE] = (kn * rs * gkn[:, :LANE]).astype(BF16)
        k_ref[:, hd * HEAD_PAD + LANE:(hd + 1) * HEAD_PAD] = (krg * rs).astype(BF16)
        v_ref[:, hd * V_DIM:(hd + 1) * V_DIM] = kv[:, hd * HEAD_PAD + LANE:(hd + 1) * HEAD_PAD].astype(BF16)


def _cache_kv(cache_ckv, cache_kr_pad, W):
    return pl.pallas_call(
        _kvc_kernel,
        grid=(DEPTH, DEC_BATCH),
        in_specs=[
            pl.BlockSpec((None, None, PAST_LEN, KV_LORA), lambda l, b: (b, l, 0, 0)),
            pl.BlockSpec((None, None, PAST_LEN, LANE), lambda l, b: (b, l, 0, 0)),
            pl.BlockSpec((None, KV_LORA, N_HEADS * HEAD_PAD), lambda l, b: (l, 0, 0)),
            pl.BlockSpec((None, 1, HEAD_PAD), lambda l, b: (l, 0, 0)),
        ],
        out_specs=[
            pl.BlockSpec((None, None, PAST_LEN, N_HEADS * HEAD_PAD), lambda l, b: (l, b, 0, 0)),
            pl.BlockSpec((None, None, PAST_LEN, N_HEADS * V_DIM), lambda l, b: (l, b, 0, 0)),
        ],
        out_shape=[
            jax.ShapeDtypeStruct((DEPTH, DEC_BATCH, PAST_LEN, N_HEADS * HEAD_PAD), BF16),
            jax.ShapeDtypeStruct((DEPTH, DEC_BATCH, PAST_LEN, N_HEADS * V_DIM), BF16),
        ],
        compiler_params=_params(("parallel", "parallel")),
        name="cache_kv",
    )(cache_ckv, cache_kr_pad, W["w_ukv"], W["g_kn"])


def _softmax_pv(q, ks, vs):
    ss = [lax.dot_general(q, k, (((1,), (1,)), ((), ())), preferred_element_type=F32) for k in ks]
    m = functools.reduce(jnp.maximum, [jnp.max(s, axis=-1, keepdims=True) for s in ss])
    ps = [jnp.exp2(s - m) for s in ss]
    den = functools.reduce(jnp.add, [jnp.sum(p, axis=-1, keepdims=True) for p in ps])
    o = functools.reduce(jnp.add, [jnp.dot(p.astype(BF16), v, preferred_element_type=F32) for p, v in zip(ps, vs)])
    return o / den


def _attn_ctx_kernel(q_ref, k_ref, v_ref, o_ref):
    for hd in range(N_HEADS):
        q = q_ref[:, hd * HEAD_PAD:(hd + 1) * HEAD_PAD]
        k = k_ref[:, hd * HEAD_PAD:(hd + 1) * HEAD_PAD]
        v = v_ref[:, hd * V_DIM:(hd + 1) * V_DIM]
        o_ref[:, hd * V_DIM:(hd + 1) * V_DIM] = _softmax_pv(q, [k], [v]).astype(BF16)


def _attn_ctx(l, q, k, v):
    return pl.pallas_call(
        _attn_ctx_kernel,
        grid=(BATCH,),
        in_specs=[
            pl.BlockSpec((SEQ, N_HEADS * HEAD_PAD), lambda b: (b, 0)),
            pl.BlockSpec((SEQ, N_HEADS * HEAD_PAD), lambda b: (b, 0)),
            pl.BlockSpec((SEQ, N_HEADS * V_DIM), lambda b: (b, 0)),
        ],
        out_specs=pl.BlockSpec((SEQ, N_HEADS * V_DIM), lambda b: (b, 0)),
        out_shape=jax.ShapeDtypeStruct((N_CTX, N_HEADS * V_DIM), BF16),
        compiler_params=_params(("parallel",)),
        name=f"attn_ctx_{l}",
    )(q, k, v)


def _attn_dec_kernel(q_ref, k_ref, v_ref, kc_ref, vc_ref, wg_ref, wu_ref, wd_ref, o_ref, wgb_ref, wub_ref, wdb_ref):
    ks = [k_ref[...], kc_ref[...]]
    vs = [v_ref[...], vc_ref[...]]
    for r in range(0, DEC_SEQ, TQ_DEC):
        o_ref[r:r + TQ_DEC, :] = _softmax_pv(q_ref[r:r + TQ_DEC, :], ks, vs).astype(BF16)
    wgb_ref[...] = wg_ref[...].astype(BF16)
    wub_ref[...] = wu_ref[...].astype(BF16)
    wdb_ref[...] = wd_ref[...].astype(BF16)


def _attn_dec(l, q, k, v, kc, vc, w_gate, w_up, w_down):
    first = N_CTX // DEC_SEQ
    steps = DEC_BATCH * N_HEADS
    slab = lambda rows, cols: pl.BlockSpec((None, rows // steps, cols), lambda b, h: (l, b * N_HEADS + h, 0))
    slab_out = lambda rows, cols: pl.BlockSpec((rows // steps, cols), lambda b, h: (b * N_HEADS + h, 0))
    return pl.pallas_call(
        _attn_dec_kernel,
        grid=(DEC_BATCH, N_HEADS),
        in_specs=[
            pl.BlockSpec((DEC_SEQ, HEAD_PAD), lambda b, h: (first + b, h)),
            pl.BlockSpec((DEC_SEQ, HEAD_PAD), lambda b, h: (first + b, h)),
            pl.BlockSpec((DEC_SEQ, V_DIM), lambda b, h: (first + b, h)),
            pl.BlockSpec((None, None, PAST_LEN, HEAD_PAD), lambda b, h: (l, b, 0, h)),
            pl.BlockSpec((None, None, PAST_LEN, V_DIM), lambda b, h: (l, b, 0, h)),
            slab(D_MODEL, D_FF), slab(D_MODEL, D_FF), slab(D_FF, D_MODEL),
        ],
        out_specs=[pl.BlockSpec((DEC_SEQ, V_DIM), lambda b, h: (b, h)),
                   slab_out(D_MODEL, D_FF), slab_out(D_MODEL, D_FF), slab_out(D_FF, D_MODEL)],
        out_shape=[jax.ShapeDtypeStruct((N_DEC, N_HEADS * V_DIM), BF16),
                   jax.ShapeDtypeStruct((D_MODEL, D_FF), BF16), jax.ShapeDtypeStruct((D_MODEL, D_FF), BF16),
                   jax.ShapeDtypeStruct((D_FF, D_MODEL), BF16)],
        compiler_params=_params(("parallel", "parallel")),
        name=f"attn_dec_{l}",
    )(q, k, v, kc, vc, w_gate, w_up, w_down)


def _mix_kernel(pin_ref, cv_ref, wp_ref, ps_ref, cw_ref, o_ref):
    i = pl.program_id(0)
    seq = jnp.where(i < N_CTX // TS_MIX, SEQ, DEC_SEQ)
    pos = lax.broadcasted_iota(jnp.int32, (TS_MIX, LANE), 0) & (seq - 1)

    def prev(x, d):
        return jnp.where(pos >= d, pltpu.roll(x, d, 0), 0.0)

    def nxt(x, d):
        return jnp.where(pos + d < seq, pltpu.roll(x, TS_MIX - d, 0), 0.0)

    for g, w in enumerate(POOL_WINDOWS):
        half = w // 2
        u = pin_ref[:, g * POOL_GC:(g + 1) * POOL_GC]
        fwd = u
        bwd = u
        span = 1
        while span < half:
            fwd = fwd + nxt(fwd, span)
            bwd = bwd + prev(bwd, span)
            span *= 2
        total = fwd + prev(bwd, 1)
        cnt = (jnp.minimum(pos + half, seq) - jnp.maximum(pos - half, 0)).astype(F32)
        d = total / cnt - u
        y = jnp.dot(d.astype(BF16), wp_ref[g], preferred_element_type=F32)
        o_ref[:, g * POOL_GC:(g + 1) * POOL_GC] = (y * ps_ref[:, g * POOL_GC:(g + 1) * POOL_GC]).astype(BF16)

    for g in range(CONV_W // LANE):
        sl = slice(g * LANE, (g + 1) * LANE)
        ch = cv_ref[:, g * LANE:(g + 1) * LANE]
        cb = cv_ref[:, CONV_W + g * LANE:CONV_W + (g + 1) * LANE]
        cc = cv_ref[:, 2 * CONV_W + g * LANE:2 * CONV_W + (g + 1) * LANE]
        z = cc * ch
        conv = prev(z, 1) * cw_ref[0:1, sl] + z * cw_ref[1:2, sl] + nxt(z, 1) * cw_ref[2:3, sl]
        o_ref[:, POOL_W + g * LANE:POOL_W + (g + 1) * LANE] = (cb * conv).astype(BF16)


def _mixers(l, pin, cv, W):
    return pl.pallas_call(
        _mix_kernel,
        grid=(N_TOK // TS_MIX,),
        in_specs=[
            pl.BlockSpec((TS_MIX, POOL_W), lambda i: (i, 0)),
            pl.BlockSpec((TS_MIX, 3 * CONV_W), lambda i: (i, 0)),
            pl.BlockSpec((None, len(POOL_WINDOWS), POOL_GC, POOL_GC), lambda i: (l, 0, 0, 0)),
            pl.BlockSpec((None, 1, POOL_W), lambda i: (l, 0, 0)),
            pl.BlockSpec((None, 3, CONV_W), lambda i: (l, 0, 0)),
        ],
        out_specs=pl.BlockSpec((TS_MIX, POOL_W + CONV_W), lambda i: (i, 0)),
        out_shape=jax.ShapeDtypeStruct((N_TOK, POOL_W + CONV_W), BF16),
        compiler_params=_params(("parallel",)),
        name=f"mixers_{l}",
    )(pin, cv, W["w_pool"], W["pool_scale"], W["conv_w"])


def _out_kernel(x_ref, actx_ref, adec_ref, pc_ref, wo_ref, g1_ref, sh_ref, sc_ref, gffn_ref, o_ref, h_ref):
    i = pl.program_id(0)
    att_w = N_HEADS * V_DIM

    def finish(att_ref):
        for r0 in range(0, TM_OUT, SUB_OUT):
            rows = slice(r0, r0 + SUB_OUT)
            y = jnp.dot(att_ref[rows, :], wo_ref[:att_w, :], preferred_element_type=F32)
            y = y + jnp.dot(pc_ref[rows, :], wo_ref[att_w:, :], preferred_element_type=F32)
            x = x_ref[rows, :] + g1_ref[...] * y
            o_ref[rows, :] = x
            h = x * _rms(x, D_MODEL) * gffn_ref[...]
            h_ref[rows, :] = (h * (1.0 + sc_ref[...]) + sh_ref[...]).astype(BF16)

    @pl.when(i < N_CTX // TM_OUT)
    def _():
        finish(actx_ref)

    @pl.when(i >= N_CTX // TM_OUT)
    def _():
        finish(adec_ref)


def _out_proj(l, x_all, att_ctx, att_dec, pc, mod4, W):
    tm = TM_OUT
    n_ctx_tiles = N_CTX // tm
    row = functools.partial(_mod_row, tm=tm)
    att_w = N_HEADS * V_DIM
    return pl.pallas_call(
        _out_kernel,
        grid=(N_TOK // tm,),
        in_specs=[
            pl.BlockSpec((tm, D_MODEL), lambda i: (i, 0)),
            pl.BlockSpec((tm, att_w), lambda i: (jnp.minimum(i, n_ctx_tiles - 1), 0)),
            pl.BlockSpec((tm, att_w), lambda i: (jnp.maximum(i - n_ctx_tiles, 0), 0)),
            pl.BlockSpec((tm, POOL_W + CONV_W), lambda i: (i, 0)),
            pl.BlockSpec((None, D_MODEL, D_MODEL), lambda i: (l, 0, 0), pipeline_mode=pl.Buffered(1)),
            pl.BlockSpec((None, None, 1, D_MODEL), lambda i: (l, row(i), 0, 2)),
            pl.BlockSpec((None, None, 1, D_MODEL), lambda i: (l, row(i), 0, 3)),
            pl.BlockSpec((None, None, 1, D_MODEL), lambda i: (l, row(i), 0, 4)),
            pl.BlockSpec((None, 1, D_MODEL), lambda i: (l, 0, 0)),
        ],
        out_specs=[pl.BlockSpec((tm, D_MODEL), lambda i: (i, 0)), pl.BlockSpec((tm, D_MODEL), lambda i: (i, 0))],
        out_shape=[jax.ShapeDtypeStruct((N_TOK, D_MODEL), F32), jax.ShapeDtypeStruct((N_TOK, D_MODEL), BF16)],
        compiler_params=_params(("parallel",)),
        name=f"out_proj_{l}",
    )(x_all, att_ctx, att_dec, pc, W["w_out"], mod4, mod4, mod4, W["g_ffn"])


def _ffn_kernel(x_ref, h_ref, g2_ref, wg_ref, wu_ref, wd_ref, *out_refs, split):
    f = pl.program_id(1)
    o_ref = out_refs[-1]

    @pl.when(f == 0)
    def _():
        o_ref[...] = x_ref[...]

    h = h_ref[...]
    y = None
    for c0 in range(0, TF_FFN, SUB_FFN):
        gate = jnp.dot(h, wg_ref[:, c0:c0 + SUB_FFN], preferred_element_type=F32)
        up = jnp.dot(h, wu_ref[:, c0:c0 + SUB_FFN], preferred_element_type=F32)
        part = jnp.dot((_silu(gate) * up).astype(BF16), wd_ref[c0:c0 + SUB_FFN, :], preferred_element_type=F32)
        y = part if y is None else y + part
    o_ref[...] += g2_ref[...] * y

    if split:
        octx_ref, odec_ref, _ = out_refs
        last = f == D_FF // TF_FFN - 1
        is_ctx = pl.program_id(0) < N_CTX // TM_FFN_SPLIT

        @pl.when(last & is_ctx)
        def _():
            octx_ref[...] = o_ref[...]

        @pl.when(last & jnp.logical_not(is_ctx))
        def _():
            odec_ref[...] = o_ref[...]


def _ffn(l, x_all, h_all, mod4, ffn_w, split=False):
    tm, tf = (TM_FFN_SPLIT if split else TM_FFN), TF_FFN
    n_ctx_tiles = N_CTX // tm
    row = functools.partial(_mod_row, tm=tm)
    if split:
        out_specs = [pl.BlockSpec((tm, D_MODEL), lambda i, f: (jnp.minimum(i, n_ctx_tiles - 1), 0)),
                     pl.BlockSpec((tm, D_MODEL), lambda i, f: (jnp.maximum(i - n_ctx_tiles, 0), 0))]
        out_shape = [jax.ShapeDtypeStruct((N_CTX, D_MODEL), F32), jax.ShapeDtypeStruct((N_DEC, D_MODEL), F32)]
        scratch = [pltpu.VMEM((tm, D_MODEL), F32)]
    else:
        out_specs = pl.BlockSpec((tm, D_MODEL), lambda i, f: (i, 0))
        out_shape = jax.ShapeDtypeStruct((N_TOK, D_MODEL), F32)
        scratch = []
    return pl.pallas_call(
        functools.partial(_ffn_kernel, split=split),
        grid=(N_TOK // tm, D_FF // tf),
        in_specs=[
            pl.BlockSpec((tm, D_MODEL), lambda i, f: (i, 0)),
            pl.BlockSpec((tm, D_MODEL), lambda i, f: (i, 0)),
            pl.BlockSpec((None, None, 1, D_MODEL), lambda i, f: (l, row(i), 0, 5)),
            pl.BlockSpec((D_MODEL, tf), lambda i, f: (0, f)),
            pl.BlockSpec((D_MODEL, tf), lambda i, f: (0, f)),
            pl.BlockSpec((tf, D_MODEL), lambda i, f: (f, 0)),
        ],
        out_specs=out_specs,
        out_shape=out_shape,
        scratch_shapes=scratch,
        compiler_params=_params(("arbitrary", "arbitrary") if split else ("parallel", "arbitrary"), VMEM_LIMIT_FFN),
        name=f"ffn_{l}",
    )(x_all, h_all, mod4, *ffn_w)


def _prep_weights(g_mix, g_ffn, w_in, g_q_a, w_uq, g_kv_a, w_ukv, g_qn, g_kn, w_pool, pool_scale, conv_w, w_out):
    w_in_a = jnp.pad(w_in[..., :IN_A_SRC].astype(BF16), ((0, 0), (0, 0), (0, IN_A - IN_A_SRC)))
    w_in_b = w_in[..., IN_A_SRC:].astype(BF16)
    w_uq_p = jnp.pad(w_uq.reshape(DEPTH, Q_LORA, N_HEADS, QK_DIM), ((0, 0), (0, 0), (0, 0), (0, HEAD_PAD - QK_DIM)))
    head_vec = lambda g: jnp.pad(g, ((0, 0), (0, HEAD_PAD - QK_DIM))).reshape(DEPTH, 1, HEAD_PAD)
    cos, sin = _rope_tables()
    return dict(
        g_mix=g_mix.reshape(DEPTH, 1, D_MODEL), g_ffn=g_ffn.reshape(DEPTH, 1, D_MODEL),
        w_in_a=w_in_a, w_in_b=w_in_b, g_q_a=g_q_a.reshape(DEPTH, 1, Q_LORA),
        w_uq=w_uq_p.reshape(DEPTH, Q_LORA, N_HEADS * HEAD_PAD).astype(BF16),
        g_kv_a=g_kv_a.reshape(DEPTH, 1, KV_LORA), w_ukv=w_ukv.astype(BF16),
        g_qn=head_vec(g_qn), g_kn=head_vec(g_kn), w_pool=w_pool.astype(BF16),
        pool_scale=pool_scale.reshape(DEPTH, 1, POOL_W), conv_w=conv_w,
        w_out=w_out.astype(BF16), cos=cos, sin=sin)


def kernel(x_prompt, x_sample, cache_ckv, cache_krope, c, c_ctx, w_ada, b_ada, g_mix, g_ffn, w_in, g_q_a, w_uq,
           g_kv_a, w_ukv, g_qn, g_kn, w_pool, pool_scale, conv_w, w_out, w_gate, w_up, w_down):
    W = _prep_weights(g_mix, g_ffn, w_in, g_q_a, w_uq, g_kv_a, w_ukv, g_qn, g_kn, w_pool, pool_scale, conv_w, w_out)
    m_all = jnp.concatenate([c_ctx[None, :], c, jnp.zeros((MOD_ROWS - 1 - DEC_BATCH, D_MODEL), F32)], axis=0)
    mod4 = _ada(m_all, w_ada, b_ada).reshape(DEPTH, MOD_ROWS, 1, N_MOD)
    cache_kr_pad = jnp.pad(cache_krope, ((0, 0), (0, 0), (0, 0), (0, LANE - ROPE_DIM)))
    kc, vc = _cache_kv(cache_ckv, cache_kr_pad, W)

    x_all = jnp.concatenate([x_prompt.reshape(N_CTX, D_MODEL), x_sample.reshape(N_DEC, D_MODEL)], axis=0)
    new_ckv, new_kr = [], []
    for l in range(DEPTH):
        q, k, v, ckvn, kr, pin, cv = _in_proj(l, x_all, mod4, W)
        att_ctx = _attn_ctx(l, q, k, v)
        att_dec, *ffn_w = _attn_dec(l, q, k, v, kc, vc, w_gate, w_up, w_down)
        pc = _mixers(l, pin, cv, W)
        x_all, h_all = _out_proj(l, x_all, att_ctx, att_dec, pc, mod4, W)
        if l < DEPTH - 1:
            x_all = _ffn(l, x_all, h_all, mod4, ffn_w)
        else:
            y_ctx, y_dec = _ffn(l, x_all, h_all, mod4, ffn_w, split=True)
        new_ckv.append(ckvn[:N_CTX].reshape(BATCH, SEQ, KV_LORA))
        new_kr.append(kr[:N_CTX].reshape(BATCH, SEQ, ROPE_DIM))
    y_prompt = y_ctx.reshape(BATCH, SEQ, D_MODEL)
    y_sample = y_dec.reshape(DEC_BATCH, DEC_SEQ, D_MODEL)
    return y_prompt, y_sample, jnp.stack(new_ckv, axis=1), jnp.stack(new_kr, axis=1)
```

```python
import functools

import jax
import jax.numpy as jnp
from jax import lax
from jax.experimental import pallas as pl
from jax.experimental.pallas import tpu as pltpu

D_MODEL = 2048
BATCH = 16
SEQ = 256
DEPTH = 4
DEC_BATCH = 4
DEC_SEQ = 2048
PAST_LEN = 512
GRID_W = 64
N_HEADS = 8
QK_NOPE = 128
ROPE_DIM = 64
QK_DIM = QK_NOPE + ROPE_DIM
V_DIM = 128
Q_LORA = 512
KV_LORA = 256
POOL_WINDOWS = (2, 4, 8, 16)
POOL_GC = 128
POOL_W = 512
CONV_W = 512
D_FF = 5632
ROPE_BASE = 10000.0
EPS = 1e-6

N_CTX = BATCH * SEQ
N_DEC = DEC_BATCH * DEC_SEQ
N_TOK = N_CTX + N_DEC
N_MOD = 6 * D_MODEL
MOD_ROWS = 8

LANE = 128
HEAD_PAD = 2 * LANE
IN_W = Q_LORA + KV_LORA + ROPE_DIM + POOL_W + 3 * CONV_W
IN_CQ = 0
IN_CKV = IN_CQ + Q_LORA
IN_KR = IN_CKV + KV_LORA
IN_A = IN_KR + LANE
IN_A_SRC = IN_KR + ROPE_DIM
IN_B = POOL_W + 3 * CONV_W

TM_IN = 512
SUB_IN = 256
TM_OUT = 512
SUB_OUT = 256
TM_FFN = 1024
TM_FFN_SPLIT = 512
TF_FFN = 512
SUB_FFN = 256
TN_ADA = 1024
TS_MIX = 2048
TQ_DEC = 256
TR_WIN = 256
VMEM_LIMIT = 56 * 1024 * 1024
VMEM_LIMIT_BIG = 62 * 1024 * 1024

F32 = jnp.float32
BF16 = jnp.bfloat16
Q_SCALE = QK_DIM ** -0.5 * 1.4426950408889634


def _mod_row(i, tm):
    n_ctx_tiles = N_CTX // tm
    return jnp.where(i < n_ctx_tiles, 0, 1 + (i - n_ctx_tiles) // (DEC_SEQ // tm))


def _params(sem, vmem=VMEM_LIMIT):
    return pltpu.CompilerParams(dimension_semantics=sem, vmem_limit_bytes=vmem)


def _silu(x):
    return x / (1.0 + jnp.exp(-x))


def _rms(x, n):
    return lax.rsqrt(jnp.sum(x * x, axis=-1, keepdims=True) * (1.0 / n) + EPS)


def _token_specs(tm, width, split):
    if not split:
        return [pl.BlockSpec((tm, width), lambda i, *_: (i, 0))]
    n_ctx_tiles = N_CTX // tm
    return [pl.BlockSpec((tm, width), lambda i, *_: (jnp.minimum(i, n_ctx_tiles - 1), 0)),
            pl.BlockSpec((tm, width), lambda i, *_: (jnp.maximum(i - n_ctx_tiles, 0), 0))]


def _token_rows(x_refs, rows, tm):
    if len(x_refs) == 1:
        return x_refs[0][rows, :]
    return jnp.where(pl.program_id(0) < N_CTX // tm, x_refs[0][rows, :], x_refs[1][rows, :])


def _ada_kernel(m_ref, w_ref, b_ref, o_ref):
    a = _silu(m_ref[...]).astype(BF16)
    o_ref[...] = jnp.dot(a, w_ref[...].astype(BF16), preferred_element_type=F32) + b_ref[...]


def _ada(m_all, w_ada, b_ada):
    return pl.pallas_call(
        _ada_kernel,
        grid=(DEPTH, N_MOD // TN_ADA),
        in_specs=[
            pl.BlockSpec((MOD_ROWS, D_MODEL), lambda l, j: (0, 0)),
            pl.BlockSpec((None, D_MODEL, TN_ADA), lambda l, j: (l, 0, j)),
            pl.BlockSpec((None, 1, TN_ADA), lambda l, j: (l, 0, j)),
        ],
        out_specs=pl.BlockSpec((None, MOD_ROWS, TN_ADA), lambda l, j: (l, 0, j)),
        out_shape=jax.ShapeDtypeStruct((DEPTH, MOD_ROWS, N_MOD), F32),
        compiler_params=_params(("parallel", "parallel")),
        name="ada_mod",
    )(m_all, w_ada, b_ada.reshape(DEPTH, 1, N_MOD))


def _w_in_kernel(w_ref, a_ref, b_ref):
    w = w_ref[...]
    a_ref[:, :IN_A_SRC] = w[:, :IN_A_SRC].astype(BF16)
    a_ref[:, IN_A_SRC:] = jnp.zeros((TR_WIN, IN_A - IN_A_SRC), BF16)
    b_ref[...] = w[:, IN_A_SRC:].astype(BF16)


def _prep_w_in(w_in):
    return pl.pallas_call(
        _w_in_kernel,
        grid=(DEPTH, D_MODEL // TR_WIN),
        in_specs=[pl.BlockSpec((None, TR_WIN, IN_W), lambda l, r: (l, r, 0))],
        out_specs=[pl.BlockSpec((None, TR_WIN, IN_A), lambda l, r: (l, r, 0)),
                   pl.BlockSpec((None, TR_WIN, IN_B), lambda l, r: (l, r, 0))],
        out_shape=[jax.ShapeDtypeStruct((DEPTH, D_MODEL, IN_A), BF16),
                   jax.ShapeDtypeStruct((DEPTH, D_MODEL, IN_B), BF16)],
        compiler_params=_params(("parallel", "parallel")),
        name="prep_w_in",
    )(w_in)


def _rope_tables():
    t = jnp.arange(DEC_SEQ)
    r_pos = (t // GRID_W).astype(F32)
    c_pos = (t % GRID_W).astype(F32)
    nf = ROPE_DIM // 4
    inv = ROPE_BASE ** (-jnp.arange(nf, dtype=F32) / nf)
    ang_r = r_pos[:, None] * inv[None, :]
    ang_c = c_pos[:, None] * inv[None, :]
    zeros = jnp.zeros((DEC_SEQ, LANE - ROPE_DIM), F32)
    cos = jnp.concatenate([jnp.cos(ang_r), jnp.cos(ang_r), jnp.cos(ang_c), jnp.cos(ang_c), zeros], axis=1)
    sin = jnp.concatenate([-jnp.sin(ang_r), jnp.sin(ang_r), -jnp.sin(ang_c), jnp.sin(ang_c), zeros], axis=1)
    ident_cos = jnp.concatenate([jnp.ones((TM_IN, ROPE_DIM), F32), jnp.zeros((TM_IN, LANE - ROPE_DIM), F32)], axis=1)
    ident_sin = jnp.zeros((TM_IN, LANE), F32)
    return jnp.concatenate([ident_cos, cos], axis=0), jnp.concatenate([ident_sin, sin], axis=0)


def _rope(x, cos, sin, first_half):
    n = x.shape[-1]
    swapped = jnp.where(first_half, pltpu.roll(x, n - ROPE_DIM // 4, 1), pltpu.roll(x, ROPE_DIM // 4, 1))
    return x * cos + swapped * sin


def _in_kernel(*refs, n_x):
    x_refs, refs = refs[:n_x], refs[n_x:]
    (sh_ref, sc_ref, gmix_ref, wa_ref, wb_ref, gqa_ref, wuq_ref, gqn_ref, gkva_ref, wukv_ref, gkn_ref,
     cos_ref, sin_ref, q_ref, k_ref, v_ref, ckv_ref, kr_ref, pin_ref, cv_ref) = refs
    lane = lax.broadcasted_iota(jnp.int32, (SUB_IN, LANE), 1)
    first_half = (lane & (ROPE_DIM // 4)) == 0
    gqn = gqn_ref[...]
    gkn = gkn_ref[...]

    for r0 in range(0, TM_IN, SUB_IN):
        rows = slice(r0, r0 + SUB_IN)
        x = _token_rows(x_refs, rows, TM_IN)
        h = x * _rms(x, D_MODEL) * gmix_ref[...]
        h = h * (1.0 + sc_ref[...]) + sh_ref[...]
        hb = h.astype(BF16)
        u = jnp.dot(hb, wa_ref[...], preferred_element_type=F32)
        um = jnp.dot(hb, wb_ref[...], preferred_element_type=F32)
        cos = cos_ref[rows, :]
        sin = sin_ref[rows, :]

        cq = u[:, IN_CQ:IN_CQ + Q_LORA]
        cqn = cq * _rms(cq, Q_LORA) * gqa_ref[...]
        q = jnp.dot(cqn.astype(BF16), wuq_ref[...], preferred_element_type=F32)
        for hd in range(N_HEADS):
            qh = q[:, hd * HEAD_PAD:(hd + 1) * HEAD_PAD]
            qh = qh * (_rms(qh, QK_DIM) * Q_SCALE) * gqn
            q_ref[rows, hd * HEAD_PAD:hd * HEAD_PAD + LANE] = qh[:, :LANE].astype(BF16)
            q_ref[rows, hd * HEAD_PAD + LANE:(hd + 1) * HEAD_PAD] = (
                _rope(qh[:, LANE:], cos, sin, first_half).astype(BF16))

        ckv = u[:, IN_CKV:IN_CKV + KV_LORA]
        ckvn = ckv * _rms(ckv, KV_LORA) * gkva_ref[...]
        ckv_ref[rows, :] = ckvn
        kv = jnp.dot(ckvn.astype(BF16), wukv_ref[...], preferred_element_type=F32)
        krb = u[:, IN_KR:IN_KR + LANE]
        kr_ref[rows, :] = krb[:, :ROPE_DIM]
        kr_ss = jnp.sum(krb * krb, axis=-1, keepdims=True)
        krg = _rope(krb * gkn[:, LANE:], cos, sin, first_half)
        for hd in range(N_HEADS):
            kn = kv[:, hd * HEAD_PAD:hd * HEAD_PAD + LANE]
            rs = lax.rsqrt((jnp.sum(kn * kn, axis=-1, keepdims=True) + kr_ss) * (1.0 / QK_DIM) + EPS)
            k_ref[rows, hd * HEAD_PAD:hd * HEAD_PAD + LANE] = (kn * rs * gkn[:, :LANE]).astype(BF16)
            k_ref[rows, hd * HEAD_PAD + LANE:(hd + 1) * HEAD_PAD] = (krg * rs).astype(BF16)
            v_ref[rows, hd * V_DIM:(hd + 1) * V_DIM] = kv[:, hd * HEAD_PAD + LANE:(hd + 1) * HEAD_PAD].astype(BF16)

        pin_ref[rows, :] = um[:, :POOL_W].astype(BF16)
        cv_ref[rows, :] = um[:, POOL_W:].astype(BF16)


def _in_proj(l, xs, mod4, W):
    tm = TM_IN
    n_ctx_tiles = N_CTX // tm
    row = functools.partial(_mod_row, tm=tm)
    vec = lambda w: pl.BlockSpec((None, 1, w), lambda i: (l, 0, 0))
    mat = lambda r, c: pl.BlockSpec((None, r, c), lambda i: (l, 0, 0), pipeline_mode=pl.Buffered(1))
    tab = pl.BlockSpec((tm, LANE), lambda i: (jnp.where(i < n_ctx_tiles, 0, 1 + (i - n_ctx_tiles) % (DEC_SEQ // tm)), 0))
    tok = lambda w: pl.BlockSpec((tm, w), lambda i: (i, 0))
    return pl.pallas_call(
        functools.partial(_in_kernel, n_x=len(xs)),
        grid=(N_TOK // tm,),
        in_specs=_token_specs(tm, D_MODEL, len(xs) == 2) + [
            pl.BlockSpec((None, None, 1, D_MODEL), lambda i: (l, row(i), 0, 0)),
            pl.BlockSpec((None, None, 1, D_MODEL), lambda i: (l, row(i), 0, 1)),
            vec(D_MODEL), mat(D_MODEL, IN_A), mat(D_MODEL, IN_B), vec(Q_LORA), mat(Q_LORA, N_HEADS * HEAD_PAD),
            vec(HEAD_PAD),
            vec(KV_LORA), mat(KV_LORA, N_HEADS * HEAD_PAD), vec(HEAD_PAD), tab, tab,
        ],
        out_specs=[tok(N_HEADS * HEAD_PAD), tok(N_HEADS * HEAD_PAD), tok(N_HEADS * V_DIM), tok(KV_LORA),
                   tok(ROPE_DIM), tok(POOL_W), tok(3 * CONV_W)],
        out_shape=[
            jax.ShapeDtypeStruct((N_TOK, N_HEADS * HEAD_PAD), BF16),
            jax.ShapeDtypeStruct((N_TOK, N_HEADS * HEAD_PAD), BF16),
            jax.ShapeDtypeStruct((N_TOK, N_HEADS * V_DIM), BF16),
            jax.ShapeDtypeStruct((N_TOK, KV_LORA), F32),
            jax.ShapeDtypeStruct((N_TOK, ROPE_DIM), F32),
            jax.ShapeDtypeStruct((N_TOK, POOL_W), BF16),
            jax.ShapeDtypeStruct((N_TOK, 3 * CONV_W), BF16),
        ],
        compiler_params=_params(("parallel",), VMEM_LIMIT_BIG if len(xs) == 2 else VMEM_LIMIT),
        name=f"in_proj_{l}",
    )(*xs, mod4, mod4, W["g_mix"], W["w_in_a"], W["w_in_b"], W["g_q_a"], W["w_uq"], W["g_qn"], W["g_kv_a"],
      W["w_ukv"], W["g_kn"], W["cos"], W["sin"])


def _kvc_kernel(ckv_ref, kr_ref, wukv_ref, gkn_ref, k_ref, v_ref):
    kv = jnp.dot(ckv_ref[...].astype(BF16), wukv_ref[...], preferred_element_type=F32)
    krb = kr_ref[...]
    kr_ss = jnp.sum(krb * krb, axis=-1, keepdims=True)
    gkn = gkn_ref[...]
    krg = krb * gkn[:, LANE:]
    for hd in range(N_HEADS):
        kn = kv[:, hd * HEAD_PAD:hd * HEAD_PAD + LANE]
        rs = lax.rsqrt((jnp.sum(kn * kn, axis=-1, keepdims=True) + kr_ss) * (1.0 / QK_DIM) + EPS)
        k_ref[:, hd * HEAD_PAD:hd * HEAD_PAD + LANE] = (kn * rs * gkn[:, :LANE]).astype(BF16)
        k_ref[:, hd * HEAD_PAD + LANE:(hd + 1) * HEAD_PAD] = (krg * rs).astype(BF16)
        v_ref[:, hd * V_DIM:(hd + 1) * V_DIM] = kv[:, hd * HEAD_PAD + LANE:(hd + 1) * HEAD_PAD].astype(BF16)


def _cache_kv(cache_ckv, cache_kr_pad, W):
    return pl.pallas_call(
        _kvc_kernel,
        grid=(DEPTH, DEC_BATCH),
        in_specs=[
            pl.BlockSpec((None, None, PAST_LEN, KV_LORA), lambda l, b: (b, l, 0, 0)),
            pl.BlockSpec((None, None, PAST_LEN, LANE), lambda l, b: (b, l, 0, 0)),
            pl.BlockSpec((None, KV_LORA, N_HEADS * HEAD_PAD), lambda l, b: (l, 0, 0)),
            pl.BlockSpec((None, 1, HEAD_PAD), lambda l, b: (l, 0, 0)),
        ],
        out_specs=[
            pl.BlockSpec((None, None, PAST_LEN, N_HEADS * HEAD_PAD), lambda l, b: (l, b, 0, 0)),
            pl.BlockSpec((None, None, PAST_LEN, N_HEADS * V_DIM), lambda l, b: (l, b, 0, 0)),
        ],
        out_shape=[
            jax.ShapeDtypeStruct((DEPTH, DEC_BATCH, PAST_LEN, N_HEADS * HEAD_PAD), BF16),
            jax.ShapeDtypeStruct((DEPTH, DEC_BATCH, PAST_LEN, N_HEADS * V_DIM), BF16),
        ],
        compiler_params=_params(("parallel", "parallel")),
        name="cache_kv",
    )(cache_ckv, cache_kr_pad, W["w_ukv"], W["g_kn"])


def _softmax_pv(q, ks, vs):
    ss = [lax.dot_general(q, k, (((1,), (1,)), ((), ())), preferred_element_type=F32) for k in ks]
    m = functools.reduce(jnp.maximum, [jnp.max(s, axis=-1, keepdims=True) for s in ss])
    ps = [jnp.exp2(s - m) for s in ss]
    den = functools.reduce(jnp.add, [jnp.sum(p, axis=-1, keepdims=True) for p in ps])
    o = functools.reduce(jnp.add, [jnp.dot(p.astype(BF16), v, preferred_element_type=F32) for p, v in zip(ps, vs)])
    return o / den


def _attn_ctx_kernel(q_ref, k_ref, v_ref, o_ref):
    for hd in range(N_HEADS):
        q = q_ref[:, hd * HEAD_PAD:(hd + 1) * HEAD_PAD]
        k = k_ref[:, hd * HEAD_PAD:(hd + 1) * HEAD_PAD]
        v = v_ref[:, hd * V_DIM:(hd + 1) * V_DIM]
        o_ref[:, hd * V_DIM:(hd + 1) * V_DIM] = _softmax_pv(q, [k], [v]).astype(BF16)


def _attn_ctx(l, q, k, v):
    return pl.pallas_call(
        _attn_ctx_kernel,
        grid=(BATCH,),
        in_specs=[
            pl.BlockSpec((SEQ, N_HEADS * HEAD_PAD), lambda b: (b, 0)),
            pl.BlockSpec((SEQ, N_HEADS * HEAD_PAD), lambda b: (b, 0)),
            pl.BlockSpec((SEQ, N_HEADS * V_DIM), lambda b: (b, 0)),
        ],
        out_specs=pl.BlockSpec((SEQ, N_HEADS * V_DIM), lambda b: (b, 0)),
        out_shape=jax.ShapeDtypeStruct((N_CTX, N_HEADS * V_DIM), BF16),
        compiler_params=_params(("parallel",)),
        name=f"attn_ctx_{l}",
    )(q, k, v)


def _attn_dec_kernel(q_ref, k_ref, v_ref, kc_ref, vc_ref, wg_ref, wu_ref, wd_ref, o_ref, wgb_ref, wub_ref, wdb_ref):
    ks = [k_ref[...], kc_ref[...]]
    vs = [v_ref[...], vc_ref[...]]
    for r in range(0, DEC_SEQ, TQ_DEC):
        o_ref[r:r + TQ_DEC, :] = _softmax_pv(q_ref[r:r + TQ_DEC, :], ks, vs).astype(BF16)
    wgb_ref[...] = wg_ref[...].astype(BF16)
    wub_ref[...] = wu_ref[...].astype(BF16)
    wdb_ref[...] = wd_ref[...].astype(BF16)


def _attn_dec(l, q, k, v, kc, vc, w_gate, w_up, w_down):
    first = N_CTX // DEC_SEQ
    steps = DEC_BATCH * N_HEADS
    slab = lambda rows, cols: pl.BlockSpec((None, rows // steps, cols), lambda b, h: (l, b * N_HEADS + h, 0))
    slab_out = lambda rows, cols: pl.BlockSpec((rows // steps, cols), lambda b, h: (b * N_HEADS + h, 0))
    return pl.pallas_call(
        _attn_dec_kernel,
        grid=(DEC_BATCH, N_HEADS),
        in_specs=[
            pl.BlockSpec((DEC_SEQ, HEAD_PAD), lambda b, h: (first + b, h)),
            pl.BlockSpec((DEC_SEQ, HEAD_PAD), lambda b, h: (first + b, h)),
            pl.BlockSpec((DEC_SEQ, V_DIM), lambda b, h: (first + b, h)),
            pl.BlockSpec((None, None, PAST_LEN, HEAD_PAD), lambda b, h: (l, b, 0, h)),
            pl.BlockSpec((None, None, PAST_LEN, V_DIM), lambda b, h: (l, b, 0, h)),
            slab(D_MODEL, D_FF), slab(D_MODEL, D_FF), slab(D_FF, D_MODEL),
        ],
        out_specs=[pl.BlockSpec((DEC_SEQ, V_DIM), lambda b, h: (b, h)),
                   slab_out(D_MODEL, D_FF), slab_out(D_MODEL, D_FF), slab_out(D_FF, D_MODEL)],
        out_shape=[jax.ShapeDtypeStruct((N_DEC, N_HEADS * V_DIM), BF16),
                   jax.ShapeDtypeStruct((D_MODEL, D_FF), BF16), jax.ShapeDtypeStruct((D_MODEL, D_FF), BF16),
                   jax.ShapeDtypeStruct((D_FF, D_MODEL), BF16)],
        compiler_params=_params(("parallel", "parallel")),
        name=f"attn_dec_{l}",
    )(q, k, v, kc, vc, w_gate, w_up, w_down)


def _mix_kernel(pin_ref, cv_ref, wp_ref, ps_ref, cw_ref, o_ref):
    i = pl.program_id(0)
    seq = jnp.where(i < N_CTX // TS_MIX, SEQ, DEC_SEQ)
    pos = lax.broadcasted_iota(jnp.int32, (TS_MIX, LANE), 0) & (seq - 1)

    def prev(x, d):
        return jnp.where(pos >= d, pltpu.roll(x, d, 0), 0.0)

    def nxt(x, d):
        return jnp.where(pos + d < seq, pltpu.roll(x, TS_MIX - d, 0), 0.0)

    for g, w in enumerate(POOL_WINDOWS):
        half = w // 2
        u = pin_ref[:, g * POOL_GC:(g + 1) * POOL_GC].astype(F32)
        fwd = u
        bwd = u
        span = 1
        while span < half:
            fwd = fwd + nxt(fwd, span)
            bwd = bwd + prev(bwd, span)
            span *= 2
        total = fwd + prev(bwd, 1)
        cnt = (jnp.minimum(pos + half, seq) - jnp.maximum(pos - half, 0)).astype(F32)
        d = total / cnt - u
        y = jnp.dot(d.astype(BF16), wp_ref[g], preferred_element_type=F32)
        o_ref[:, g * POOL_GC:(g + 1) * POOL_GC] = (y * ps_ref[:, g * POOL_GC:(g + 1) * POOL_GC]).astype(BF16)

    for g in range(CONV_W // LANE):
        sl = slice(g * LANE, (g + 1) * LANE)
        ch = cv_ref[:, g * LANE:(g + 1) * LANE].astype(F32)
        cb = cv_ref[:, CONV_W + g * LANE:CONV_W + (g + 1) * LANE].astype(F32)
        cc = cv_ref[:, 2 * CONV_W + g * LANE:2 * CONV_W + (g + 1) * LANE].astype(F32)
        z = cc * ch
        conv = prev(z, 1) * cw_ref[0:1, sl] + z * cw_ref[1:2, sl] + nxt(z, 1) * cw_ref[2:3, sl]
        o_ref[:, POOL_W + g * LANE:POOL_W + (g + 1) * LANE] = (cb * conv).astype(BF16)


def _mixers(l, pin, cv, W):
    return pl.pallas_call(
        _mix_kernel,
        grid=(N_TOK // TS_MIX,),
        in_specs=[
            pl.BlockSpec((TS_MIX, POOL_W), lambda i: (i, 0)),
            pl.BlockSpec((TS_MIX, 3 * CONV_W), lambda i: (i, 0)),
            pl.BlockSpec((None, len(POOL_WINDOWS), POOL_GC, POOL_GC), lambda i: (l, 0, 0, 0)),
            pl.BlockSpec((None, 1, POOL_W), lambda i: (l, 0, 0)),
            pl.BlockSpec((None, 3, CONV_W), lambda i: (l, 0, 0)),
        ],
        out_specs=pl.BlockSpec((TS_MIX, POOL_W + CONV_W), lambda i: (i, 0)),
        out_shape=jax.ShapeDtypeStruct((N_TOK, POOL_W + CONV_W), BF16),
        compiler_params=_params(("parallel",)),
        name=f"mixers_{l}",
    )(pin, cv, W["w_pool"], W["pool_scale"], W["conv_w"])


def _out_kernel(*refs, n_x):
    x_refs, refs = refs[:n_x], refs[n_x:]
    actx_ref, adec_ref, pc_ref, wo_ref, g1_ref, sh_ref, sc_ref, gffn_ref, o_ref, h_ref = refs
    i = pl.program_id(0)
    att_w = N_HEADS * V_DIM

    def finish(att_ref):
        for r0 in range(0, TM_OUT, SUB_OUT):
            rows = slice(r0, r0 + SUB_OUT)
            y = jnp.dot(att_ref[rows, :], wo_ref[:att_w, :], preferred_element_type=F32)
            y = y + jnp.dot(pc_ref[rows, :], wo_ref[att_w:, :], preferred_element_type=F32)
            x = _token_rows(x_refs, rows, TM_OUT) + g1_ref[...] * y
            o_ref[rows, :] = x
            h = x * _rms(x, D_MODEL) * gffn_ref[...]
            h_ref[rows, :] = (h * (1.0 + sc_ref[...]) + sh_ref[...]).astype(BF16)

    @pl.when(i < N_CTX // TM_OUT)
    def _():
        finish(actx_ref)

    @pl.when(i >= N_CTX // TM_OUT)
    def _():
        finish(adec_ref)


def _out_proj(l, xs, att_ctx, att_dec, pc, mod4, W):
    tm = TM_OUT
    n_ctx_tiles = N_CTX // tm
    row = functools.partial(_mod_row, tm=tm)
    att_w = N_HEADS * V_DIM
    return pl.pallas_call(
        functools.partial(_out_kernel, n_x=len(xs)),
        grid=(N_TOK // tm,),
        in_specs=_token_specs(tm, D_MODEL, len(xs) == 2) + [
            pl.BlockSpec((tm, att_w), lambda i: (jnp.minimum(i, n_ctx_tiles - 1), 0)),
            pl.BlockSpec((tm, att_w), lambda i: (jnp.maximum(i - n_ctx_tiles, 0), 0)),
            pl.BlockSpec((tm, POOL_W + CONV_W), lambda i: (i, 0)),
            pl.BlockSpec((None, D_MODEL, D_MODEL), lambda i: (l, 0, 0), pipeline_mode=pl.Buffered(1)),
            pl.BlockSpec((None, None, 1, D_MODEL), lambda i: (l, row(i), 0, 2)),
            pl.BlockSpec((None, None, 1, D_MODEL), lambda i: (l, row(i), 0, 3)),
            pl.BlockSpec((None, None, 1, D_MODEL), lambda i: (l, row(i), 0, 4)),
            pl.BlockSpec((None, 1, D_MODEL), lambda i: (l, 0, 0)),
        ],
        out_specs=[pl.BlockSpec((tm, D_MODEL), lambda i: (i, 0)), pl.BlockSpec((tm, D_MODEL), lambda i: (i, 0))],
        out_shape=[jax.ShapeDtypeStruct((N_TOK, D_MODEL), F32), jax.ShapeDtypeStruct((N_TOK, D_MODEL), BF16)],
        compiler_params=_params(("parallel",)),
        name=f"out_proj_{l}",
    )(*xs, att_ctx, att_dec, pc, W["w_out"], mod4, mod4, mod4, W["g_ffn"])


def _ffn_kernel(x_ref, h_ref, g2_ref, wg_ref, wu_ref, wd_ref, *out_refs, split):
    f = pl.program_id(1)
    o_ref = out_refs[-1]

    @pl.when(f == 0)
    def _():
        o_ref[...] = x_ref[...]

    h = h_ref[...]
    y = None
    for c0 in range(0, TF_FFN, SUB_FFN):
        gate = jnp.dot(h, wg_ref[:, c0:c0 + SUB_FFN], preferred_element_type=F32)
        up = jnp.dot(h, wu_ref[:, c0:c0 + SUB_FFN], preferred_element_type=F32)
        part = jnp.dot((_silu(gate) * up).astype(BF16), wd_ref[c0:c0 + SUB_FFN, :], preferred_element_type=F32)
        y = part if y is None else y + part
    o_ref[...] += g2_ref[...] * y

    if split:
        octx_ref, odec_ref, _ = out_refs
        last = f == D_FF // TF_FFN - 1
        is_ctx = pl.program_id(0) < N_CTX // TM_FFN_SPLIT

        @pl.when(last & is_ctx)
        def _():
            octx_ref[...] = o_ref[...]

        @pl.when(last & jnp.logical_not(is_ctx))
        def _():
            odec_ref[...] = o_ref[...]


def _ffn(l, x_all, h_all, mod4, ffn_w, split=False):
    tm, tf = (TM_FFN_SPLIT if split else TM_FFN), TF_FFN
    n_ctx_tiles = N_CTX // tm
    row = functools.partial(_mod_row, tm=tm)
    if split:
        out_specs = [pl.BlockSpec((tm, D_MODEL), lambda i, f: (jnp.minimum(i, n_ctx_tiles - 1), 0)),
                     pl.BlockSpec((tm, D_MODEL), lambda i, f: (jnp.maximum(i - n_ctx_tiles, 0), 0))]
        out_shape = [jax.ShapeDtypeStruct((N_CTX, D_MODEL), F32), jax.ShapeDtypeStruct((N_DEC, D_MODEL), F32)]
        scratch = [pltpu.VMEM((tm, D_MODEL), F32)]
    else:
        out_specs = pl.BlockSpec((tm, D_MODEL), lambda i, f: (i, 0))
        out_shape = jax.ShapeDtypeStruct((N_TOK, D_MODEL), F32)
        scratch = []
    return pl.pallas_call(
        functools.partial(_ffn_kernel, split=split),
        grid=(N_TOK // tm, D_FF // tf),
        in_specs=[
            pl.BlockSpec((tm, D_MODEL), lambda i, f: (i, 0)),
            pl.BlockSpec((tm, D_MODEL), lambda i, f: (i, 0)),
            pl.BlockSpec((None, None, 1, D_MODEL), lambda i, f: (l, row(i), 0, 5)),
            pl.BlockSpec((D_MODEL, tf), lambda i, f: (0, f)),
            pl.BlockSpec((D_MODEL, tf), lambda i, f: (0, f)),
            pl.BlockSpec((tf, D_MODEL), lambda i, f: (f, 0)),
        ],
        out_specs=out_specs,
        out_shape=out_shape,
        scratch_shapes=scratch,
        compiler_params=_params(("arbitrary", "arbitrary") if split else ("parallel", "arbitrary"), VMEM_LIMIT_BIG),
        name=f"ffn_{l}",
    )(x_all, h_all, mod4, *ffn_w)


def _prep_weights(g_mix, g_ffn, w_in, g_q_a, w_uq, g_kv_a, w_ukv, g_qn, g_kn, w_pool, pool_scale, conv_w, w_out):
    w_in_a, w_in_b = _prep_w_in(w_in)
    w_uq_p = jnp.pad(w_uq.reshape(DEPTH, Q_LORA, N_HEADS, QK_DIM), ((0, 0), (0, 0), (0, 0), (0, HEAD_PAD - QK_DIM)))
    head_vec = lambda g: jnp.pad(g, ((0, 0), (0, HEAD_PAD - QK_DIM))).reshape(DEPTH, 1, HEAD_PAD)
    cos, sin = _rope_tables()
    return dict(
        g_mix=g_mix.reshape(DEPTH, 1, D_MODEL), g_ffn=g_ffn.reshape(DEPTH, 1, D_MODEL),
        w_in_a=w_in_a, w_in_b=w_in_b, g_q_a=g_q_a.reshape(DEPTH, 1, Q_LORA),
        w_uq=w_uq_p.reshape(DEPTH, Q_LORA, N_HEADS * HEAD_PAD).astype(BF16),
        g_kv_a=g_kv_a.reshape(DEPTH, 1, KV_LORA), w_ukv=w_ukv.astype(BF16),
        g_qn=head_vec(g_qn), g_kn=head_vec(g_kn), w_pool=w_pool.astype(BF16),
        pool_scale=pool_scale.reshape(DEPTH, 1, POOL_W), conv_w=conv_w,
        w_out=w_out.astype(BF16), cos=cos, sin=sin)


def kernel(x_prompt, x_sample, cache_ckv, cache_krope, c, c_ctx, w_ada, b_ada, g_mix, g_ffn, w_in, g_q_a, w_uq,
           g_kv_a, w_ukv, g_qn, g_kn, w_pool, pool_scale, conv_w, w_out, w_gate, w_up, w_down):
    W = _prep_weights(g_mix, g_ffn, w_in, g_q_a, w_uq, g_kv_a, w_ukv, g_qn, g_kn, w_pool, pool_scale, conv_w, w_out)
    m_all = jnp.concatenate([c_ctx[None, :], c, jnp.zeros((MOD_ROWS - 1 - DEC_BATCH, D_MODEL), F32)], axis=0)
    mod4 = _ada(m_all, w_ada, b_ada).reshape(DEPTH, MOD_ROWS, 1, N_MOD)
    cache_kr_pad = jnp.pad(cache_krope, ((0, 0), (0, 0), (0, 0), (0, LANE - ROPE_DIM)))
    kc, vc = _cache_kv(cache_ckv, cache_kr_pad, W)

    xs = (x_prompt.reshape(N_CTX, D_MODEL), x_sample.reshape(N_DEC, D_MODEL))
    new_ckv, new_kr = [], []
    for l in range(DEPTH):
        q, k, v, ckvn, kr, pin, cv = _in_proj(l, xs, mod4, W)
        att_ctx = _attn_ctx(l, q, k, v)
        att_dec, *ffn_w = _attn_dec(l, q, k, v, kc, vc, w_gate, w_up, w_down)
        pc = _mixers(l, pin, cv, W)
        x_all, h_all = _out_proj(l, xs, att_ctx, att_dec, pc, mod4, W)
        if l < DEPTH - 1:
            xs = (_ffn(l, x_all, h_all, mod4, ffn_w),)
        else:
            y_ctx, y_dec = _ffn(l, x_all, h_all, mod4, ffn_w, split=True)
        new_ckv.append(ckvn[:N_CTX].reshape(BATCH, SEQ, KV_LORA))
        new_kr.append(kr[:N_CTX].reshape(BATCH, SEQ, ROPE_DIM))
    y_prompt = y_ctx.reshape(BATCH, SEQ, D_MODEL)
    y_sample = y_dec.reshape(DEC_BATCH, DEC_SEQ, D_MODEL)
    return y_prompt, y_sample, jnp.stack(new_ckv, axis=1), jnp.stack(new_kr, axis=1)
```

```python
import functools

import jax
import jax.numpy as jnp
from jax import lax
from jax.experimental import pallas as pl
from jax.experimental.pallas import tpu as pltpu

D_MODEL = 2048
BATCH = 16
SEQ = 256
DEPTH = 4
DEC_BATCH = 4
DEC_SEQ = 2048
PAST_LEN = 512
GRID_W = 64
N_HEADS = 8
QK_NOPE = 128
ROPE_DIM = 64
QK_DIM = QK_NOPE + ROPE_DIM
V_DIM = 128
Q_LORA = 512
KV_LORA = 256
POOL_WINDOWS = (2, 4, 8, 16)
POOL_GC = 128
POOL_W = 512
CONV_W = 512
D_FF = 5632
ROPE_BASE = 10000.0
EPS = 1e-6

N_CTX = BATCH * SEQ
N_DEC = DEC_BATCH * DEC_SEQ
N_TOK = N_CTX + N_DEC
N_MOD = 6 * D_MODEL
MOD_ROWS = 8

LANE = 128
HEAD_PAD = 2 * LANE
IN_W = Q_LORA + KV_LORA + ROPE_DIM + POOL_W + 3 * CONV_W
IN_CQ = 0
IN_CKV = IN_CQ + Q_LORA
IN_KR = IN_CKV + KV_LORA
IN_A = IN_KR + LANE
IN_A_SRC = IN_KR + ROPE_DIM
IN_B = POOL_W + 3 * CONV_W

TM_IN = 512
SUB_IN = 256
TM_OUT = 512
SUB_OUT = 256
TM_FFN = 1024
TM_FFN_SPLIT = 512
TF_FFN = 512
SUB_FFN = 256
TN_ADA = 1024
TS_MIX = 2048
HALO = 8
TQ_DEC = 256
TR_WIN = 256
VMEM_LIMIT = 56 * 1024 * 1024
VMEM_LIMIT_BIG = 62 * 1024 * 1024

F32 = jnp.float32
BF16 = jnp.bfloat16
Q_SCALE = QK_DIM ** -0.5 * 1.4426950408889634


def _mod_row(i, tm):
    n_ctx_tiles = N_CTX // tm
    return jnp.where(i < n_ctx_tiles, 0, 1 + (i - n_ctx_tiles) // (DEC_SEQ // tm))


def _params(sem, vmem=VMEM_LIMIT):
    return pltpu.CompilerParams(dimension_semantics=sem, vmem_limit_bytes=vmem)


def _silu(x):
    return x / (1.0 + jnp.exp(-x))


def _rms(x, n):
    return lax.rsqrt(jnp.sum(x * x, axis=-1, keepdims=True) * (1.0 / n) + EPS)


def _token_specs(tm, width, split):
    if not split:
        return [pl.BlockSpec((tm, width), lambda i, *_: (i, 0))]
    n_ctx_tiles = N_CTX // tm
    return [pl.BlockSpec((tm, width), lambda i, *_: (jnp.minimum(i, n_ctx_tiles - 1), 0)),
            pl.BlockSpec((tm, width), lambda i, *_: (jnp.maximum(i - n_ctx_tiles, 0), 0))]


def _token_rows(x_refs, rows, tm):
    if len(x_refs) == 1:
        return x_refs[0][rows, :]
    return jnp.where(pl.program_id(0) < N_CTX // tm, x_refs[0][rows, :], x_refs[1][rows, :])


def _ada_kernel(m_ref, w_ref, b_ref, o_ref):
    a = _silu(m_ref[...]).astype(BF16)
    o_ref[...] = jnp.dot(a, w_ref[...].astype(BF16), preferred_element_type=F32) + b_ref[...]


def _ada(m_all, w_ada, b_ada):
    return pl.pallas_call(
        _ada_kernel,
        grid=(DEPTH, N_MOD // TN_ADA),
        in_specs=[
            pl.BlockSpec((MOD_ROWS, D_MODEL), lambda l, j: (0, 0)),
            pl.BlockSpec((None, D_MODEL, TN_ADA), lambda l, j: (l, 0, j)),
            pl.BlockSpec((None, 1, TN_ADA), lambda l, j: (l, 0, j)),
        ],
        out_specs=pl.BlockSpec((None, MOD_ROWS, TN_ADA), lambda l, j: (l, 0, j)),
        out_shape=jax.ShapeDtypeStruct((DEPTH, MOD_ROWS, N_MOD), F32),
        compiler_params=_params(("parallel", "parallel")),
        name="ada_mod",
    )(m_all, w_ada, b_ada.reshape(DEPTH, 1, N_MOD))


def _w_in_kernel(wt_ref, a_ref, b_ref):
    a = wt_ref[:IN_A, :].T
    lane = lax.broadcasted_iota(jnp.int32, a.shape, 1)
    a_ref[...] = jnp.where(lane < IN_A_SRC, a, 0.0).astype(BF16)
    b_ref[...] = wt_ref[IN_A_SRC:, :].T.astype(BF16)


def _prep_w_in(w_in):
    return pl.pallas_call(
        _w_in_kernel,
        grid=(DEPTH, D_MODEL // TR_WIN),
        in_specs=[pl.BlockSpec((None, IN_W, TR_WIN), lambda l, r: (l, 0, r))],
        out_specs=[pl.BlockSpec((None, TR_WIN, IN_A), lambda l, r: (l, r, 0)),
                   pl.BlockSpec((None, TR_WIN, IN_B), lambda l, r: (l, r, 0))],
        out_shape=[jax.ShapeDtypeStruct((DEPTH, D_MODEL, IN_A), BF16),
                   jax.ShapeDtypeStruct((DEPTH, D_MODEL, IN_B), BF16)],
        compiler_params=_params(("parallel", "parallel")),
        name="prep_w_in",
    )(jnp.swapaxes(w_in, 1, 2))


def _rope_tables():
    t = jnp.arange(DEC_SEQ)
    r_pos = (t // GRID_W).astype(F32)
    c_pos = (t % GRID_W).astype(F32)
    nf = ROPE_DIM // 4
    inv = ROPE_BASE ** (-jnp.arange(nf, dtype=F32) / nf)
    ang_r = r_pos[:, None] * inv[None, :]
    ang_c = c_pos[:, None] * inv[None, :]
    zeros = jnp.zeros((DEC_SEQ, LANE - ROPE_DIM), F32)
    cos = jnp.concatenate([jnp.cos(ang_r), jnp.cos(ang_r), jnp.cos(ang_c), jnp.cos(ang_c), zeros], axis=1)
    sin = jnp.concatenate([-jnp.sin(ang_r), jnp.sin(ang_r), -jnp.sin(ang_c), jnp.sin(ang_c), zeros], axis=1)
    ident_cos = jnp.concatenate([jnp.ones((TM_IN, ROPE_DIM), F32), jnp.zeros((TM_IN, LANE - ROPE_DIM), F32)], axis=1)
    ident_sin = jnp.zeros((TM_IN, LANE), F32)
    return jnp.concatenate([ident_cos, cos], axis=0), jnp.concatenate([ident_sin, sin], axis=0)


def _rope(x, cos, sin, first_half):
    n = x.shape[-1]
    swapped = jnp.where(first_half, pltpu.roll(x, n - ROPE_DIM // 4, 1), pltpu.roll(x, ROPE_DIM // 4, 1))
    return x * cos + swapped * sin


def _in_kernel(*refs, n_x):
    x_refs, refs = refs[:n_x], refs[n_x:]
    (sh_ref, sc_ref, gmix_ref, wa_ref, wb_ref, gqa_ref, wuq_ref, gqn_ref, gkva_ref, wukv_ref, gkn_ref,
     cos_ref, sin_ref, q_ref, k_ref, v_ref, ckv_ref, kr_ref, pin_ref, cv_ref) = refs
    lane = lax.broadcasted_iota(jnp.int32, (SUB_IN, LANE), 1)
    first_half = (lane & (ROPE_DIM // 4)) == 0
    gqn = gqn_ref[...]
    gkn = gkn_ref[...]

    for r0 in range(0, TM_IN, SUB_IN):
        rows = slice(r0, r0 + SUB_IN)
        x = _token_rows(x_refs, rows, TM_IN)
        h = x * _rms(x, D_MODEL) * gmix_ref[...]
        h = h * (1.0 + sc_ref[...]) + sh_ref[...]
        hb = h.astype(BF16)
        u = jnp.dot(hb, wa_ref[...], preferred_element_type=F32)
        um = jnp.dot(hb, wb_ref[...], preferred_element_type=F32)
        cos = cos_ref[rows, :]
        sin = sin_ref[rows, :]

        cq = u[:, IN_CQ:IN_CQ + Q_LORA]
        cqn = cq * _rms(cq, Q_LORA) * gqa_ref[...]
        q = jnp.dot(cqn.astype(BF16), wuq_ref[...], preferred_element_type=F32)
        for hd in range(N_HEADS):
            qh = q[:, hd * HEAD_PAD:(hd + 1) * HEAD_PAD]
            qh = qh * (_rms(qh, QK_DIM) * Q_SCALE) * gqn
            q_ref[rows, hd * HEAD_PAD:hd * HEAD_PAD + LANE] = qh[:, :LANE].astype(BF16)
            q_ref[rows, hd * HEAD_PAD + LANE:(hd + 1) * HEAD_PAD] = (
                _rope(qh[:, LANE:], cos, sin, first_half).astype(BF16))

        ckv = u[:, IN_CKV:IN_CKV + KV_LORA]
        ckvn = ckv * _rms(ckv, KV_LORA) * gkva_ref[...]
        ckv_ref[rows, :] = ckvn
        kv = jnp.dot(ckvn.astype(BF16), wukv_ref[...], preferred_element_type=F32)
        krb = u[:, IN_KR:IN_KR + LANE]
        kr_ref[rows, :] = krb[:, :ROPE_DIM]
        kr_ss = jnp.sum(krb * krb, axis=-1, keepdims=True)
        krg = _rope(krb * gkn[:, LANE:], cos, sin, first_half)
        for hd in range(N_HEADS):
            kn = kv[:, hd * HEAD_PAD:hd * HEAD_PAD + LANE]
            rs = lax.rsqrt((jnp.sum(kn * kn, axis=-1, keepdims=True) + kr_ss) * (1.0 / QK_DIM) + EPS)
            k_ref[rows, hd * HEAD_PAD:hd * HEAD_PAD + LANE] = (kn * rs * gkn[:, :LANE]).astype(BF16)
            k_ref[rows, hd * HEAD_PAD + LANE:(hd + 1) * HEAD_PAD] = (krg * rs).astype(BF16)
            v_ref[rows, hd * V_DIM:(hd + 1) * V_DIM] = kv[:, hd * HEAD_PAD + LANE:(hd + 1) * HEAD_PAD].astype(BF16)

        pin_ref[rows, :] = um[:, :POOL_W]
        cv_ref[rows, :] = um[:, POOL_W:]


def _in_proj(l, xs, mod4, W):
    tm = TM_IN
    n_ctx_tiles = N_CTX // tm
    row = functools.partial(_mod_row, tm=tm)
    vec = lambda w: pl.BlockSpec((None, 1, w), lambda i: (l, 0, 0))
    mat = lambda r, c: pl.BlockSpec((None, r, c), lambda i: (l, 0, 0), pipeline_mode=pl.Buffered(1))
    tab = pl.BlockSpec((tm, LANE), lambda i: (jnp.where(i < n_ctx_tiles, 0, 1 + (i - n_ctx_tiles) % (DEC_SEQ // tm)), 0))
    tok = lambda w: pl.BlockSpec((tm, w), lambda i: (i, 0))
    return pl.pallas_call(
        functools.partial(_in_kernel, n_x=len(xs)),
        grid=(N_TOK // tm,),
        in_specs=_token_specs(tm, D_MODEL, len(xs) == 2) + [
            pl.BlockSpec((None, None, 1, D_MODEL), lambda i: (l, row(i), 0, 0)),
            pl.BlockSpec((None, None, 1, D_MODEL), lambda i: (l, row(i), 0, 1)),
            vec(D_MODEL), mat(D_MODEL, IN_A), mat(D_MODEL, IN_B), vec(Q_LORA), mat(Q_LORA, N_HEADS * HEAD_PAD),
            vec(HEAD_PAD),
            vec(KV_LORA), mat(KV_LORA, N_HEADS * HEAD_PAD), vec(HEAD_PAD), tab, tab,
        ],
        out_specs=[tok(N_HEADS * HEAD_PAD), tok(N_HEADS * HEAD_PAD), tok(N_HEADS * V_DIM), tok(KV_LORA),
                   tok(ROPE_DIM), tok(POOL_W), tok(3 * CONV_W)],
        out_shape=[
            jax.ShapeDtypeStruct((N_TOK, N_HEADS * HEAD_PAD), BF16),
            jax.ShapeDtypeStruct((N_TOK, N_HEADS * HEAD_PAD), BF16),
            jax.ShapeDtypeStruct((N_TOK, N_HEADS * V_DIM), BF16),
            jax.ShapeDtypeStruct((N_TOK, KV_LORA), F32),
            jax.ShapeDtypeStruct((N_TOK, ROPE_DIM), F32),
            jax.ShapeDtypeStruct((N_TOK, POOL_W), F32),
            jax.ShapeDtypeStruct((N_TOK, 3 * CONV_W), F32),
        ],
        compiler_params=_params(("parallel",), VMEM_LIMIT_BIG if len(xs) == 2 else VMEM_LIMIT),
        name=f"in_proj_{l}",
    )(*xs, mod4, mod4, W["g_mix"], W["w_in_a"], W["w_in_b"], W["g_q_a"], W["w_uq"], W["g_qn"], W["g_kv_a"],
      W["w_ukv"], W["g_kn"], W["cos"], W["sin"])


def _kvc_kernel(ckv_ref, kr_ref, wukv_ref, gkn_ref, k_ref, v_ref):
    kv = jnp.dot(ckv_ref[...].astype(BF16), wukv_ref[...], preferred_element_type=F32)
    krb = kr_ref[...]
    kr_ss = jnp.sum(krb * krb, axis=-1, keepdims=True)
    gkn = gkn_ref[...]
    krg = krb * gkn[:, LANE:]
    for hd in range(N_HEADS):
        kn = kv[:, hd * HEAD_PAD:hd * HEAD_PAD + LANE]
        rs = lax.rsqrt((jnp.sum(kn * kn, axis=-1, keepdims=True) + kr_ss) * (1.0 / QK_DIM) + EPS)
        k_ref[:, hd * HEAD_PAD:hd * HEAD_PAD + LANE] = (kn * rs * gkn[:, :LANE]).astype(BF16)
        k_ref[:, hd * HEAD_PAD + LANE:(hd + 1) * HEAD_PAD] = (krg * rs).astype(BF16)
        v_ref[:, hd * V_DIM:(hd + 1) * V_DIM] = kv[:, hd * HEAD_PAD + LANE:(hd + 1) * HEAD_PAD].astype(BF16)


def _cache_kv(cache_ckv, cache_kr_pad, W):
    return pl.pallas_call(
        _kvc_kernel,
        grid=(DEPTH, DEC_BATCH),
        in_specs=[
            pl.BlockSpec((None, None, PAST_LEN, KV_LORA), lambda l, b: (b, l, 0, 0)),
            pl.BlockSpec((None, None, PAST_LEN, LANE), lambda l, b: (b, l, 0, 0)),
            pl.BlockSpec((None, KV_LORA, N_HEADS * HEAD_PAD), lambda l, b: (l, 0, 0)),
            pl.BlockSpec((None, 1, HEAD_PAD), lambda l, b: (l, 0, 0)),
        ],
        out_specs=[
            pl.BlockSpec((None, None, PAST_LEN, N_HEADS * HEAD_PAD), lambda l, b: (l, b, 0, 0)),
            pl.BlockSpec((None, None, PAST_LEN, N_HEADS * V_DIM), lambda l, b: (l, b, 0, 0)),
        ],
        out_shape=[
            jax.ShapeDtypeStruct((DEPTH, DEC_BATCH, PAST_LEN, N_HEADS * HEAD_PAD), BF16),
            jax.ShapeDtypeStruct((DEPTH, DEC_BATCH, PAST_LEN, N_HEADS * V_DIM), BF16),
        ],
        compiler_params=_params(("parallel", "parallel")),
        name="cache_kv",
    )(cache_ckv, cache_kr_pad, W["w_ukv"], W["g_kn"])


def _softmax_pv(q, ks, vs):
    ss = [lax.dot_general(q, k, (((1,), (1,)), ((), ())), preferred_element_type=F32) for k in ks]
    m = functools.reduce(jnp.maximum, [jnp.max(s, axis=-1, keepdims=True) for s in ss])
    ps = [jnp.exp2(s - m) for s in ss]
    den = functools.reduce(jnp.add, [jnp.sum(p, axis=-1, keepdims=True) for p in ps])
    o = functools.reduce(jnp.add, [jnp.dot(p.astype(BF16), v, preferred_element_type=F32) for p, v in zip(ps, vs)])
    return o / den


def _attn_ctx_kernel(q_ref, k_ref, v_ref, o_ref):
    for hd in range(N_HEADS):
        q = q_ref[:, hd * HEAD_PAD:(hd + 1) * HEAD_PAD]
        k = k_ref[:, hd * HEAD_PAD:(hd + 1) * HEAD_PAD]
        v = v_ref[:, hd * V_DIM:(hd + 1) * V_DIM]
        o_ref[:, hd * V_DIM:(hd + 1) * V_DIM] = _softmax_pv(q, [k], [v]).astype(BF16)


def _attn_ctx(l, q, k, v):
    return pl.pallas_call(
        _attn_ctx_kernel,
        grid=(BATCH,),
        in_specs=[
            pl.BlockSpec((SEQ, N_HEADS * HEAD_PAD), lambda b: (b, 0)),
            pl.BlockSpec((SEQ, N_HEADS * HEAD_PAD), lambda b: (b, 0)),
            pl.BlockSpec((SEQ, N_HEADS * V_DIM), lambda b: (b, 0)),
        ],
        out_specs=pl.BlockSpec((SEQ, N_HEADS * V_DIM), lambda b: (b, 0)),
        out_shape=jax.ShapeDtypeStruct((N_CTX, N_HEADS * V_DIM), BF16),
        compiler_params=_params(("parallel",)),
        name=f"attn_ctx_{l}",
    )(q, k, v)


def _attn_dec_kernel(q_ref, k_ref, v_ref, kc_ref, vc_ref, wg_ref, wu_ref, wd_ref, o_ref, wgb_ref, wub_ref, wdb_ref):
    ks = [k_ref[...], kc_ref[...]]
    vs = [v_ref[...], vc_ref[...]]
    for r in range(0, DEC_SEQ, TQ_DEC):
        o_ref[r:r + TQ_DEC, :] = _softmax_pv(q_ref[r:r + TQ_DEC, :], ks, vs).astype(BF16)
    wgb_ref[...] = wg_ref[...].astype(BF16)
    wub_ref[...] = wu_ref[...].astype(BF16)
    wdb_ref[...] = wd_ref[...].astype(BF16)


def _attn_dec(l, q, k, v, kc, vc, w_gate, w_up, w_down):
    first = N_CTX // DEC_SEQ
    steps = DEC_BATCH * N_HEADS
    slab = lambda rows, cols: pl.BlockSpec((None, rows // steps, cols), lambda b, h: (l, b * N_HEADS + h, 0))
    slab_out = lambda rows, cols: pl.BlockSpec((rows // steps, cols), lambda b, h: (b * N_HEADS + h, 0))
    return pl.pallas_call(
        _attn_dec_kernel,
        grid=(DEC_BATCH, N_HEADS),
        in_specs=[
            pl.BlockSpec((DEC_SEQ, HEAD_PAD), lambda b, h: (first + b, h)),
            pl.BlockSpec((DEC_SEQ, HEAD_PAD), lambda b, h: (first + b, h)),
            pl.BlockSpec((DEC_SEQ, V_DIM), lambda b, h: (first + b, h)),
            pl.BlockSpec((None, None, PAST_LEN, HEAD_PAD), lambda b, h: (l, b, 0, h)),
            pl.BlockSpec((None, None, PAST_LEN, V_DIM), lambda b, h: (l, b, 0, h)),
            slab(D_MODEL, D_FF), slab(D_MODEL, D_FF), slab(D_FF, D_MODEL),
        ],
        out_specs=[pl.BlockSpec((DEC_SEQ, V_DIM), lambda b, h: (b, h)),
                   slab_out(D_MODEL, D_FF), slab_out(D_MODEL, D_FF), slab_out(D_FF, D_MODEL)],
        out_shape=[jax.ShapeDtypeStruct((N_DEC, N_HEADS * V_DIM), BF16),
                   jax.ShapeDtypeStruct((D_MODEL, D_FF), BF16), jax.ShapeDtypeStruct((D_MODEL, D_FF), BF16),
                   jax.ShapeDtypeStruct((D_FF, D_MODEL), BF16)],
        compiler_params=_params(("parallel", "parallel")),
        name=f"attn_dec_{l}",
    )(q, k, v, kc, vc, w_gate, w_up, w_down)


def _mix_block(pin_ref, cv_ref, wp_ref, ps_ref, cw_ref, o_ref, pad_ref, seq):
    n_seq = TS_MIX // seq
    zeros = jnp.zeros((n_seq, HALO, LANE), F32)
    pad_ref[:, :HALO, :] = zeros
    pad_ref[:, HALO + seq:, :] = zeros
    pos = lax.broadcasted_iota(jnp.int32, (seq, LANE), 0)

    def put(x):
        pad_ref[:, HALO:HALO + seq, :] = x.reshape(n_seq, seq, LANE)

    def at(j):
        return pad_ref[:, HALO + j:HALO + j + seq, :]

    for g, w in enumerate(POOL_WINDOWS):
        half = w // 2
        u = pin_ref[:, g * POOL_GC:(g + 1) * POOL_GC]
        put(u)
        total = functools.reduce(jnp.add, [at(j) for j in range(-half, half)])
        cnt = (jnp.minimum(pos + half, seq) - jnp.maximum(pos - half, 0)).astype(F32)
        d = (total / cnt).reshape(TS_MIX, LANE) - u
        y = jnp.dot(d.astype(BF16), wp_ref[g], preferred_element_type=F32)
        o_ref[:, g * POOL_GC:(g + 1) * POOL_GC] = (y * ps_ref[:, g * POOL_GC:(g + 1) * POOL_GC]).astype(BF16)

    for g in range(CONV_W // LANE):
        sl = slice(g * LANE, (g + 1) * LANE)
        ch = cv_ref[:, g * LANE:(g + 1) * LANE]
        cb = cv_ref[:, CONV_W + g * LANE:CONV_W + (g + 1) * LANE]
        cc = cv_ref[:, 2 * CONV_W + g * LANE:2 * CONV_W + (g + 1) * LANE]
        put(cc * ch)
        conv = at(-1) * cw_ref[0:1, sl] + at(0) * cw_ref[1:2, sl] + at(1) * cw_ref[2:3, sl]
        o_ref[:, POOL_W + g * LANE:POOL_W + (g + 1) * LANE] = (cb * conv.reshape(TS_MIX, LANE)).astype(BF16)


def _mix_kernel(pin_ref, cv_ref, wp_ref, ps_ref, cw_ref, o_ref, pad_ctx_ref, pad_dec_ref):
    i = pl.program_id(0)

    @pl.when(i < N_CTX // TS_MIX)
    def _():
        _mix_block(pin_ref, cv_ref, wp_ref, ps_ref, cw_ref, o_ref, pad_ctx_ref, SEQ)

    @pl.when(i >= N_CTX // TS_MIX)
    def _():
        _mix_block(pin_ref, cv_ref, wp_ref, ps_ref, cw_ref, o_ref, pad_dec_ref, DEC_SEQ)


def _mixers(l, pin, cv, W):
    return pl.pallas_call(
        _mix_kernel,
        grid=(N_TOK // TS_MIX,),
        in_specs=[
            pl.BlockSpec((TS_MIX, POOL_W), lambda i: (i, 0)),
            pl.BlockSpec((TS_MIX, 3 * CONV_W), lambda i: (i, 0)),
            pl.BlockSpec((None, len(POOL_WINDOWS), POOL_GC, POOL_GC), lambda i: (l, 0, 0, 0)),
            pl.BlockSpec((None, 1, POOL_W), lambda i: (l, 0, 0)),
            pl.BlockSpec((None, 3, CONV_W), lambda i: (l, 0, 0)),
        ],
        out_specs=pl.BlockSpec((TS_MIX, POOL_W + CONV_W), lambda i: (i, 0)),
        out_shape=jax.ShapeDtypeStruct((N_TOK, POOL_W + CONV_W), BF16),
        scratch_shapes=[pltpu.VMEM((TS_MIX // SEQ, SEQ + 2 * HALO, LANE), F32),
                        pltpu.VMEM((TS_MIX // DEC_SEQ, DEC_SEQ + 2 * HALO, LANE), F32)],
        compiler_params=_params(("parallel",), VMEM_LIMIT_BIG),
        name=f"mixers_{l}",
    )(pin, cv, W["w_pool"], W["pool_scale"], W["conv_w"])


def _out_kernel(*refs, n_x):
    x_refs, refs = refs[:n_x], refs[n_x:]
    actx_ref, adec_ref, pc_ref, wo_ref, g1_ref, sh_ref, sc_ref, gffn_ref, o_ref, h_ref = refs
    i = pl.program_id(0)
    att_w = N_HEADS * V_DIM

    def finish(att_ref):
        for r0 in range(0, TM_OUT, SUB_OUT):
            rows = slice(r0, r0 + SUB_OUT)
            y = jnp.dot(att_ref[rows, :], wo_ref[:att_w, :], preferred_element_type=F32)
            y = y + jnp.dot(pc_ref[rows, :], wo_ref[att_w:, :], preferred_element_type=F32)
            x = _token_rows(x_refs, rows, TM_OUT) + g1_ref[...] * y
            o_ref[rows, :] = x
            h = x * _rms(x, D_MODEL) * gffn_ref[...]
            h_ref[rows, :] = (h * (1.0 + sc_ref[...]) + sh_ref[...]).astype(BF16)

    @pl.when(i < N_CTX // TM_OUT)
    def _():
        finish(actx_ref)

    @pl.when(i >= N_CTX // TM_OUT)
    def _():
        finish(adec_ref)


def _out_proj(l, xs, att_ctx, att_dec, pc, mod4, W):
    tm = TM_OUT
    n_ctx_tiles = N_CTX // tm
    row = functools.partial(_mod_row, tm=tm)
    att_w = N_HEADS * V_DIM
    return pl.pallas_call(
        functools.partial(_out_kernel, n_x=len(xs)),
        grid=(N_TOK // tm,),
        in_specs=_token_specs(tm, D_MODEL, len(xs) == 2) + [
            pl.BlockSpec((tm, att_w), lambda i: (jnp.minimum(i, n_ctx_tiles - 1), 0)),
            pl.BlockSpec((tm, att_w), lambda i: (jnp.maximum(i - n_ctx_tiles, 0), 0)),
            pl.BlockSpec((tm, POOL_W + CONV_W), lambda i: (i, 0)),
            pl.BlockSpec((None, D_MODEL, D_MODEL), lambda i: (l, 0, 0), pipeline_mode=pl.Buffered(1)),
            pl.BlockSpec((None, None, 1, D_MODEL), lambda i: (l, row(i), 0, 2)),
            pl.BlockSpec((None, None, 1, D_MODEL), lambda i: (l, row(i), 0, 3)),
            pl.BlockSpec((None, None, 1, D_MODEL), lambda i: (l, row(i), 0, 4)),
            pl.BlockSpec((None, 1, D_MODEL), lambda i: (l, 0, 0)),
        ],
        out_specs=[pl.BlockSpec((tm, D_MODEL), lambda i: (i, 0)), pl.BlockSpec((tm, D_MODEL), lambda i: (i, 0))],
        out_shape=[jax.ShapeDtypeStruct((N_TOK, D_MODEL), F32), jax.ShapeDtypeStruct((N_TOK, D_MODEL), BF16)],
        compiler_params=_params(("parallel",)),
        name=f"out_proj_{l}",
    )(*xs, att_ctx, att_dec, pc, W["w_out"], mod4, mod4, mod4, W["g_ffn"])


def _ffn_kernel(x_ref, h_ref, g2_ref, wg_ref, wu_ref, wd_ref, *out_refs, split):
    f = pl.program_id(1)
    o_ref = out_refs[-1]

    @pl.when(f == 0)
    def _():
        o_ref[...] = x_ref[...]

    h = h_ref[...]
    y = None
    for c0 in range(0, TF_FFN, SUB_FFN):
        gate = jnp.dot(h, wg_ref[:, c0:c0 + SUB_FFN], preferred_element_type=F32)
        up = jnp.dot(h, wu_ref[:, c0:c0 + SUB_FFN], preferred_element_type=F32)
        part = jnp.dot((_silu(gate) * up).astype(BF16), wd_ref[c0:c0 + SUB_FFN, :], preferred_element_type=F32)
        y = part if y is None else y + part
    o_ref[...] += g2_ref[...] * y

    if split:
        octx_ref, odec_ref, _ = out_refs
        last = f == D_FF // TF_FFN - 1
        is_ctx = pl.program_id(0) < N_CTX // TM_FFN_SPLIT

        @pl.when(last & is_ctx)
        def _():
            octx_ref[...] = o_ref[...]

        @pl.when(last & jnp.logical_not(is_ctx))
        def _():
            odec_ref[...] = o_ref[...]


def _ffn(l, x_all, h_all, mod4, ffn_w, split=False):
    tm, tf = (TM_FFN_SPLIT if split else TM_FFN), TF_FFN
    n_ctx_tiles = N_CTX // tm
    row = functools.partial(_mod_row, tm=tm)
    if split:
        out_specs = [pl.BlockSpec((tm, D_MODEL), lambda i, f: (jnp.minimum(i, n_ctx_tiles - 1), 0)),
                     pl.BlockSpec((tm, D_MODEL), lambda i, f: (jnp.maximum(i - n_ctx_tiles, 0), 0))]
        out_shape = [jax.ShapeDtypeStruct((N_CTX, D_MODEL), F32), jax.ShapeDtypeStruct((N_DEC, D_MODEL), F32)]
        scratch = [pltpu.VMEM((tm, D_MODEL), F32)]
    else:
        out_specs = pl.BlockSpec((tm, D_MODEL), lambda i, f: (i, 0))
        out_shape = jax.ShapeDtypeStruct((N_TOK, D_MODEL), F32)
        scratch = []
    return pl.pallas_call(
        functools.partial(_ffn_kernel, split=split),
        grid=(N_TOK // tm, D_FF // tf),
        in_specs=[
            pl.BlockSpec((tm, D_MODEL), lambda i, f: (i, 0)),
            pl.BlockSpec((tm, D_MODEL), lambda i, f: (i, 0)),
            pl.BlockSpec((None, None, 1, D_MODEL), lambda i, f: (l, row(i), 0, 5)),
            pl.BlockSpec((D_MODEL, tf), lambda i, f: (0, f)),
            pl.BlockSpec((D_MODEL, tf), lambda i, f: (0, f)),
            pl.BlockSpec((tf, D_MODEL), lambda i, f: (f, 0)),
        ],
        out_specs=out_specs,
        out_shape=out_shape,
        scratch_shapes=scratch,
        compiler_params=_params(("arbitrary", "arbitrary") if split else ("parallel", "arbitrary"), VMEM_LIMIT_BIG),
        name=f"ffn_{l}",
    )(x_all, h_all, mod4, *ffn_w)


def _prep_weights(g_mix, g_ffn, w_in, g_q_a, w_uq, g_kv_a, w_ukv, g_qn, g_kn, w_pool, pool_scale, conv_w, w_out):
    w_in_a, w_in_b = _prep_w_in(w_in)
    w_uq_p = jnp.pad(w_uq.reshape(DEPTH, Q_LORA, N_HEADS, QK_DIM), ((0, 0), (0, 0), (0, 0), (0, HEAD_PAD - QK_DIM)))
    head_vec = lambda g: jnp.pad(g, ((0, 0), (0, HEAD_PAD - QK_DIM))).reshape(DEPTH, 1, HEAD_PAD)
    cos, sin = _rope_tables()
    return dict(
        g_mix=g_mix.reshape(DEPTH, 1, D_MODEL), g_ffn=g_ffn.reshape(DEPTH, 1, D_MODEL),
        w_in_a=w_in_a, w_in_b=w_in_b, g_q_a=g_q_a.reshape(DEPTH, 1, Q_LORA),
        w_uq=w_uq_p.reshape(DEPTH, Q_LORA, N_HEADS * HEAD_PAD).astype(BF16),
        g_kv_a=g_kv_a.reshape(DEPTH, 1, KV_LORA), w_ukv=w_ukv.astype(BF16),
        g_qn=head_vec(g_qn), g_kn=head_vec(g_kn), w_pool=w_pool.astype(BF16),
        pool_scale=pool_scale.reshape(DEPTH, 1, POOL_W), conv_w=conv_w,
        w_out=w_out.astype(BF16), cos=cos, sin=sin)


def kernel(x_prompt, x_sample, cache_ckv, cache_krope, c, c_ctx, w_ada, b_ada, g_mix, g_ffn, w_in, g_q_a, w_uq,
           g_kv_a, w_ukv, g_qn, g_kn, w_pool, pool_scale, conv_w, w_out, w_gate, w_up, w_down):
    W = _prep_weights(g_mix, g_ffn, w_in, g_q_a, w_uq, g_kv_a, w_ukv, g_qn, g_kn, w_pool, pool_scale, conv_w, w_out)
    m_all = jnp.concatenate([c_ctx[None, :], c, jnp.zeros((MOD_ROWS - 1 - DEC_BATCH, D_MODEL), F32)], axis=0)
    mod4 = _ada(m_all, w_ada, b_ada).reshape(DEPTH, MOD_ROWS, 1, N_MOD)
    cache_kr_pad = jnp.pad(cache_krope, ((0, 0), (0, 0), (0, 0), (0, LANE - ROPE_DIM)))
    kc, vc = _cache_kv(cache_ckv, cache_kr_pad, W)

    xs = (x_prompt.reshape(N_CTX, D_MODEL), x_sample.reshape(N_DEC, D_MODEL))
    new_ckv, new_kr = [], []
    for l in range(DEPTH):
        q, k, v, ckvn, kr, pin, cv = _in_proj(l, xs, mod4, W)
        att_ctx = _attn_ctx(l, q, k, v)
        att_dec, *ffn_w = _attn_dec(l, q, k, v, kc, vc, w_gate, w_up, w_down)
        pc = _mixers(l, pin, cv, W)
        x_all, h_all = _out_proj(l, xs, att_ctx, att_dec, pc, mod4, W)
        if l < DEPTH - 1:
            xs = (_ffn(l, x_all, h_all, mod4, ffn_w),)
        else:
            y_ctx, y_dec = _ffn(l, x_all, h_all, mod4, ffn_w, split=True)
        new_ckv.append(ckvn[:N_CTX].reshape(BATCH, SEQ, KV_LORA))
        new_kr.append(kr[:N_CTX].reshape(BATCH, SEQ, ROPE_DIM))
    y_prompt = y_ctx.reshape(BATCH, SEQ, D_MODEL)
    y_sample = y_dec.reshape(DEC_BATCH, DEC_SEQ, D_MODEL)
    return y_prompt, y_sample, jnp.stack(new_ckv, axis=1), jnp.stack(new_kr, axis=1)
```

```python
import functools

import jax
import jax.numpy as jnp
from jax import lax
from jax.experimental import pallas as pl
from jax.experimental.pallas import tpu as pltpu

D_MODEL = 2048
BATCH = 16
SEQ = 256
DEPTH = 4
DEC_BATCH = 4
DEC_SEQ = 2048
PAST_LEN = 512
GRID_W = 64
N_HEADS = 8
QK_NOPE = 128
ROPE_DIM = 64
QK_DIM = QK_NOPE + ROPE_DIM
V_DIM = 128
Q_LORA = 512
KV_LORA = 256
POOL_WINDOWS = (2, 4, 8, 16)
POOL_GC = 128
POOL_W = 512
CONV_W = 512
D_FF = 5632
ROPE_BASE = 10000.0
EPS = 1e-6

N_CTX = BATCH * SEQ
N_DEC = DEC_BATCH * DEC_SEQ
N_TOK = N_CTX + N_DEC
N_MOD = 6 * D_MODEL
MOD_ROWS = 8

LANE = 128
HEAD_PAD = 2 * LANE
IN_W = Q_LORA + KV_LORA + ROPE_DIM + POOL_W + 3 * CONV_W
IN_CQ = 0
IN_CKV = IN_CQ + Q_LORA
IN_KR = IN_CKV + KV_LORA
IN_A = IN_KR + LANE
IN_A_SRC = IN_KR + ROPE_DIM
IN_B = POOL_W + 3 * CONV_W

TM_IN = 512
SUB_IN = 256
TM_OUT = 512
SUB_OUT = 256
TM_FFN = 1024
TF_FFN = 512
SUB_FFN = 256
TN_ADA = 1024
TS_MIX = 2048
HALO = 8
TQ_DEC = 256
TR_WIN = 256
VMEM_LIMIT = 56 * 1024 * 1024
VMEM_LIMIT_BIG = 62 * 1024 * 1024

F32 = jnp.float32
BF16 = jnp.bfloat16
Q_SCALE = QK_DIM ** -0.5 * 1.4426950408889634


def _mod_row(i, tm):
    n_ctx_tiles = N_CTX // tm
    return jnp.where(i < n_ctx_tiles, 0, 1 + (i - n_ctx_tiles) // (DEC_SEQ // tm))


def _params(sem, vmem=VMEM_LIMIT):
    return pltpu.CompilerParams(dimension_semantics=sem, vmem_limit_bytes=vmem)


def _silu(x):
    return x / (1.0 + jnp.exp(-x))


def _rms(x, n):
    return lax.rsqrt(jnp.sum(x * x, axis=-1, keepdims=True) * (1.0 / n) + EPS)


def _token_specs(tm, width, split):
    if not split:
        return [pl.BlockSpec((tm, width), lambda i, *_: (i, 0))]
    n_ctx_tiles = N_CTX // tm
    return [pl.BlockSpec((tm, width), lambda i, *_: (jnp.minimum(i, n_ctx_tiles - 1), 0)),
            pl.BlockSpec((tm, width), lambda i, *_: (jnp.maximum(i - n_ctx_tiles, 0), 0))]


def _token_rows(x_refs, rows, tm):
    if len(x_refs) == 1:
        return x_refs[0][rows, :]
    return jnp.where(pl.program_id(0) < N_CTX // tm, x_refs[0][rows, :], x_refs[1][rows, :])


def _ada_kernel(m_ref, w_ref, b_ref, o_ref):
    a = _silu(m_ref[...]).astype(BF16)
    o_ref[...] = jnp.dot(a, w_ref[...].astype(BF16), preferred_element_type=F32) + b_ref[...]


def _ada(m_all, w_ada, b_ada):
    return pl.pallas_call(
        _ada_kernel,
        grid=(DEPTH, N_MOD // TN_ADA),
        in_specs=[
            pl.BlockSpec((MOD_ROWS, D_MODEL), lambda l, j: (0, 0)),
            pl.BlockSpec((None, D_MODEL, TN_ADA), lambda l, j: (l, 0, j)),
            pl.BlockSpec((None, 1, TN_ADA), lambda l, j: (l, 0, j)),
        ],
        out_specs=pl.BlockSpec((None, MOD_ROWS, TN_ADA), lambda l, j: (l, 0, j)),
        out_shape=jax.ShapeDtypeStruct((DEPTH, MOD_ROWS, N_MOD), F32),
        compiler_params=_params(("parallel", "parallel")),
        name="ada_mod",
    )(m_all, w_ada, b_ada.reshape(DEPTH, 1, N_MOD))


def _w_in_kernel(wt_ref, a_ref, b_ref):
    a = wt_ref[:IN_A, :].T
    lane = lax.broadcasted_iota(jnp.int32, a.shape, 1)
    a_ref[...] = jnp.where(lane < IN_A_SRC, a, 0.0).astype(BF16)
    b_ref[...] = wt_ref[IN_A_SRC:, :].T.astype(BF16)


def _prep_w_in(w_in):
    return pl.pallas_call(
        _w_in_kernel,
        grid=(DEPTH, D_MODEL // TR_WIN),
        in_specs=[pl.BlockSpec((None, IN_W, TR_WIN), lambda l, r: (l, 0, r))],
        out_specs=[pl.BlockSpec((None, TR_WIN, IN_A), lambda l, r: (l, r, 0)),
                   pl.BlockSpec((None, TR_WIN, IN_B), lambda l, r: (l, r, 0))],
        out_shape=[jax.ShapeDtypeStruct((DEPTH, D_MODEL, IN_A), BF16),
                   jax.ShapeDtypeStruct((DEPTH, D_MODEL, IN_B), BF16)],
        compiler_params=_params(("parallel", "parallel")),
        name="prep_w_in",
    )(jnp.swapaxes(w_in, 1, 2))


def _rope_tables():
    t = jnp.arange(DEC_SEQ)
    r_pos = (t // GRID_W).astype(F32)
    c_pos = (t % GRID_W).astype(F32)
    nf = ROPE_DIM // 4
    inv = ROPE_BASE ** (-jnp.arange(nf, dtype=F32) / nf)
    ang_r = r_pos[:, None] * inv[None, :]
    ang_c = c_pos[:, None] * inv[None, :]
    zeros = jnp.zeros((DEC_SEQ, LANE - ROPE_DIM), F32)
    cos = jnp.concatenate([jnp.cos(ang_r), jnp.cos(ang_r), jnp.cos(ang_c), jnp.cos(ang_c), zeros], axis=1)
    sin = jnp.concatenate([-jnp.sin(ang_r), jnp.sin(ang_r), -jnp.sin(ang_c), jnp.sin(ang_c), zeros], axis=1)
    ident_cos = jnp.concatenate([jnp.ones((TM_IN, ROPE_DIM), F32), jnp.zeros((TM_IN, LANE - ROPE_DIM), F32)], axis=1)
    ident_sin = jnp.zeros((TM_IN, LANE), F32)
    return jnp.concatenate([ident_cos, cos], axis=0), jnp.concatenate([ident_sin, sin], axis=0)


def _rope(x, cos, sin, first_half):
    n = x.shape[-1]
    swapped = jnp.where(first_half, pltpu.roll(x, n - ROPE_DIM // 4, 1), pltpu.roll(x, ROPE_DIM // 4, 1))
    return x * cos + swapped * sin


def _in_kernel(*refs, n_x):
    x_refs, refs = refs[:n_x], refs[n_x:]
    (sh_ref, sc_ref, gmix_ref, wa_ref, wb_ref, gqa_ref, wuq_ref, gqn_ref, gkva_ref, wukv_ref, gkn_ref,
     cos_ref, sin_ref, q_ref, k_ref, v_ref, ckv_ref, kr_ref, pin_ref, cv_ref, hb_scr) = refs
    lane = lax.broadcasted_iota(jnp.int32, (SUB_IN, LANE), 1)
    first_half = (lane & (ROPE_DIM // 4)) == 0
    gqn = gqn_ref[...]
    gkn = gkn_ref[...]

    for r0 in range(0, TM_IN, SUB_IN):
        rows = slice(r0, r0 + SUB_IN)
        x = _token_rows(x_refs, rows, TM_IN)
        h = x * _rms(x, D_MODEL) * gmix_ref[...]
        h = h * (1.0 + sc_ref[...]) + sh_ref[...]
        hb = h.astype(BF16)
        hb_scr[rows, :] = hb
        u = jnp.dot(hb, wa_ref[...], preferred_element_type=F32)
        cos = cos_ref[rows, :]
        sin = sin_ref[rows, :]

        cq = u[:, IN_CQ:IN_CQ + Q_LORA]
        cqn = cq * _rms(cq, Q_LORA) * gqa_ref[...]
        q = jnp.dot(cqn.astype(BF16), wuq_ref[...], preferred_element_type=F32)
        for hd in range(N_HEADS):
            qh = q[:, hd * HEAD_PAD:(hd + 1) * HEAD_PAD]
            qh = qh * (_rms(qh, QK_DIM) * Q_SCALE) * gqn
            q_ref[rows, hd * HEAD_PAD:hd * HEAD_PAD + LANE] = qh[:, :LANE].astype(BF16)
            q_ref[rows, hd * HEAD_PAD + LANE:(hd + 1) * HEAD_PAD] = (
                _rope(qh[:, LANE:], cos, sin, first_half).astype(BF16))

        ckv = u[:, IN_CKV:IN_CKV + KV_LORA]
        ckvn = ckv * _rms(ckv, KV_LORA) * gkva_ref[...]
        ckv_ref[rows, :] = ckvn
        kv = jnp.dot(ckvn.astype(BF16), wukv_ref[...], preferred_element_type=F32)
        krb = u[:, IN_KR:IN_KR + LANE]
        kr_ref[rows, :] = krb[:, :ROPE_DIM]
        kr_ss = jnp.sum(krb * krb, axis=-1, keepdims=True)
        krg = _rope(krb * gkn[:, LANE:], cos, sin, first_half)
        for hd in range(N_HEADS):
            kn = kv[:, hd * HEAD_PAD:hd * HEAD_PAD + LANE]
            rs = lax.rsqrt((jnp.sum(kn * kn, axis=-1, keepdims=True) + kr_ss) * (1.0 / QK_DIM) + EPS)
            k_ref[rows, hd * HEAD_PAD:hd * HEAD_PAD + LANE] = (kn * rs * gkn[:, :LANE]).astype(BF16)
            k_ref[rows, hd * HEAD_PAD + LANE:(hd + 1) * HEAD_PAD] = (krg * rs).astype(BF16)
            v_ref[rows, hd * V_DIM:(hd + 1) * V_DIM] = kv[:, hd * HEAD_PAD + LANE:(hd + 1) * HEAD_PAD].astype(BF16)

    um = jnp.dot(hb_scr[...], wb_ref[...], preferred_element_type=F32)
    pin_ref[...] = um[:, :POOL_W].astype(BF16)
    cv_ref[...] = um[:, POOL_W:].astype(BF16)


def _in_proj(l, xs, mod4, W):
    tm = TM_IN
    n_ctx_tiles = N_CTX // tm
    row = functools.partial(_mod_row, tm=tm)
    vec = lambda w: pl.BlockSpec((None, 1, w), lambda i: (l, 0, 0))
    mat = lambda r, c: pl.BlockSpec((None, r, c), lambda i: (l, 0, 0), pipeline_mode=pl.Buffered(1))
    tab = pl.BlockSpec((tm, LANE), lambda i: (jnp.where(i < n_ctx_tiles, 0, 1 + (i - n_ctx_tiles) % (DEC_SEQ // tm)), 0))
    tok = lambda w: pl.BlockSpec((tm, w), lambda i: (i, 0))
    return pl.pallas_call(
        functools.partial(_in_kernel, n_x=len(xs)),
        grid=(N_TOK // tm,),
        in_specs=_token_specs(tm, D_MODEL, len(xs) == 2) + [
            pl.BlockSpec((None, None, 1, D_MODEL), lambda i: (l, row(i), 0, 0)),
            pl.BlockSpec((None, None, 1, D_MODEL), lambda i: (l, row(i), 0, 1)),
            vec(D_MODEL), mat(D_MODEL, IN_A), mat(D_MODEL, IN_B), vec(Q_LORA), mat(Q_LORA, N_HEADS * HEAD_PAD),
            vec(HEAD_PAD),
            vec(KV_LORA), mat(KV_LORA, N_HEADS * HEAD_PAD), vec(HEAD_PAD), tab, tab,
        ],
        out_specs=[tok(N_HEADS * HEAD_PAD), tok(N_HEADS * HEAD_PAD), tok(N_HEADS * V_DIM), tok(KV_LORA),
                   tok(ROPE_DIM), tok(POOL_W), tok(3 * CONV_W)],
        out_shape=[
            jax.ShapeDtypeStruct((N_TOK, N_HEADS * HEAD_PAD), BF16),
            jax.ShapeDtypeStruct((N_TOK, N_HEADS * HEAD_PAD), BF16),
            jax.ShapeDtypeStruct((N_TOK, N_HEADS * V_DIM), BF16),
            jax.ShapeDtypeStruct((N_TOK, KV_LORA), F32),
            jax.ShapeDtypeStruct((N_TOK, ROPE_DIM), F32),
            jax.ShapeDtypeStruct((N_TOK, POOL_W), BF16),
            jax.ShapeDtypeStruct((N_TOK, 3 * CONV_W), BF16),
        ],
        scratch_shapes=[pltpu.VMEM((tm, D_MODEL), BF16)],
        compiler_params=_params(("parallel",), VMEM_LIMIT_BIG if len(xs) == 2 else VMEM_LIMIT),
        name=f"in_proj_{l}",
    )(*xs, mod4, mod4, W["g_mix"], W["w_in_a"], W["w_in_b"], W["g_q_a"], W["w_uq"], W["g_qn"], W["g_kv_a"],
      W["w_ukv"], W["g_kn"], W["cos"], W["sin"])


def _kvc_kernel(ckv_ref, kr_ref, wukv_ref, gkn_ref, k_ref, v_ref):
    kv = jnp.dot(ckv_ref[...].astype(BF16), wukv_ref[...], preferred_element_type=F32)
    krb = kr_ref[...]
    kr_ss = jnp.sum(krb * krb, axis=-1, keepdims=True)
    gkn = gkn_ref[...]
    krg = krb * gkn[:, LANE:]
    for hd in range(N_HEADS):
        kn = kv[:, hd * HEAD_PAD:hd * HEAD_PAD + LANE]
        rs = lax.rsqrt((jnp.sum(kn * kn, axis=-1, keepdims=True) + kr_ss) * (1.0 / QK_DIM) + EPS)
        k_ref[:, hd * HEAD_PAD:hd * HEAD_PAD + LANE] = (kn * rs * gkn[:, :LANE]).astype(BF16)
        k_ref[:, hd * HEAD_PAD + LANE:(hd + 1) * HEAD_PAD] = (krg * rs).astype(BF16)
        v_ref[:, hd * V_DIM:(hd + 1) * V_DIM] = kv[:, hd * HEAD_PAD + LANE:(hd + 1) * HEAD_PAD].astype(BF16)


def _cache_kv(cache_ckv, cache_kr_pad, W):
    return pl.pallas_call(
        _kvc_kernel,
        grid=(DEPTH, DEC_BATCH),
        in_specs=[
            pl.BlockSpec((None, None, PAST_LEN, KV_LORA), lambda l, b: (b, l, 0, 0)),
            pl.BlockSpec((None, None, PAST_LEN, LANE), lambda l, b: (b, l, 0, 0)),
            pl.BlockSpec((None, KV_LORA, N_HEADS * HEAD_PAD), lambda l, b: (l, 0, 0)),
            pl.BlockSpec((None, 1, HEAD_PAD), lambda l, b: (l, 0, 0)),
        ],
        out_specs=[
            pl.BlockSpec((None, None, PAST_LEN, N_HEADS * HEAD_PAD), lambda l, b: (l, b, 0, 0)),
            pl.BlockSpec((None, None, PAST_LEN, N_HEADS * V_DIM), lambda l, b: (l, b, 0, 0)),
        ],
        out_shape=[
            jax.ShapeDtypeStruct((DEPTH, DEC_BATCH, PAST_LEN, N_HEADS * HEAD_PAD), BF16),
            jax.ShapeDtypeStruct((DEPTH, DEC_BATCH, PAST_LEN, N_HEADS * V_DIM), BF16),
        ],
        compiler_params=_params(("parallel", "parallel")),
        name="cache_kv",
    )(cache_ckv, cache_kr_pad, W["w_ukv"], W["g_kn"])


def _softmax_pv(q, ks, vs):
    ss = [lax.dot_general(q, k, (((1,), (1,)), ((), ())), preferred_element_type=F32) for k in ks]
    m = functools.reduce(jnp.maximum, [jnp.max(s, axis=-1, keepdims=True) for s in ss])
    ps = [jnp.exp2(s - m) for s in ss]
    den = functools.reduce(jnp.add, [jnp.sum(p, axis=-1, keepdims=True) for p in ps])
    o = functools.reduce(jnp.add, [jnp.dot(p.astype(BF16), v, preferred_element_type=F32) for p, v in zip(ps, vs)])
    return o / den


def _attn_ctx_kernel(q_ref, k_ref, v_ref, o_ref):
    for hd in range(N_HEADS):
        q = q_ref[:, hd * HEAD_PAD:(hd + 1) * HEAD_PAD]
        k = k_ref[:, hd * HEAD_PAD:(hd + 1) * HEAD_PAD]
        v = v_ref[:, hd * V_DIM:(hd + 1) * V_DIM]
        o_ref[:, hd * V_DIM:(hd + 1) * V_DIM] = _softmax_pv(q, [k], [v]).astype(BF16)


def _attn_ctx(l, q, k, v):
    return pl.pallas_call(
        _attn_ctx_kernel,
        grid=(BATCH,),
        in_specs=[
            pl.BlockSpec((SEQ, N_HEADS * HEAD_PAD), lambda b: (b, 0)),
            pl.BlockSpec((SEQ, N_HEADS * HEAD_PAD), lambda b: (b, 0)),
            pl.BlockSpec((SEQ, N_HEADS * V_DIM), lambda b: (b, 0)),
        ],
        out_specs=pl.BlockSpec((SEQ, N_HEADS * V_DIM), lambda b: (b, 0)),
        out_shape=jax.ShapeDtypeStruct((N_CTX, N_HEADS * V_DIM), BF16),
        compiler_params=_params(("parallel",)),
        name=f"attn_ctx_{l}",
    )(q, k, v)


def _attn_dec_kernel(q_ref, k_ref, v_ref, kc_ref, vc_ref, wg_ref, wu_ref, wd_ref, wo_ref,
                     o_ref, wgb_ref, wub_ref, wdb_ref, wob_ref):
    ks = [k_ref[...], kc_ref[...]]
    vs = [v_ref[...], vc_ref[...]]
    for r in range(0, DEC_SEQ, TQ_DEC):
        o_ref[r:r + TQ_DEC, :] = _softmax_pv(q_ref[r:r + TQ_DEC, :], ks, vs).astype(BF16)
    wgb_ref[...] = wg_ref[...].astype(BF16)
    wub_ref[...] = wu_ref[...].astype(BF16)
    wdb_ref[...] = wd_ref[...].astype(BF16)
    wob_ref[...] = wo_ref[...].astype(BF16)


def _attn_dec(l, q, k, v, kc, vc, w_gate, w_up, w_down, w_out):
    first = N_CTX // DEC_SEQ
    steps = DEC_BATCH * N_HEADS
    slab = lambda rows, cols: pl.BlockSpec((None, rows // steps, cols), lambda b, h: (l, b * N_HEADS + h, 0))
    slab_out = lambda rows, cols: pl.BlockSpec((rows // steps, cols), lambda b, h: (b * N_HEADS + h, 0))
    return pl.pallas_call(
        _attn_dec_kernel,
        grid=(DEC_BATCH, N_HEADS),
        in_specs=[
            pl.BlockSpec((DEC_SEQ, HEAD_PAD), lambda b, h: (first + b, h)),
            pl.BlockSpec((DEC_SEQ, HEAD_PAD), lambda b, h: (first + b, h)),
            pl.BlockSpec((DEC_SEQ, V_DIM), lambda b, h: (first + b, h)),
            pl.BlockSpec((None, None, PAST_LEN, HEAD_PAD), lambda b, h: (l, b, 0, h)),
            pl.BlockSpec((None, None, PAST_LEN, V_DIM), lambda b, h: (l, b, 0, h)),
            slab(D_MODEL, D_FF), slab(D_MODEL, D_FF), slab(D_FF, D_MODEL), slab(D_MODEL, D_MODEL),
        ],
        out_specs=[pl.BlockSpec((DEC_SEQ, V_DIM), lambda b, h: (b, h)),
                   slab_out(D_MODEL, D_FF), slab_out(D_MODEL, D_FF), slab_out(D_FF, D_MODEL),
                   slab_out(D_MODEL, D_MODEL)],
        out_shape=[jax.ShapeDtypeStruct((N_DEC, N_HEADS * V_DIM), BF16),
                   jax.ShapeDtypeStruct((D_MODEL, D_FF), BF16), jax.ShapeDtypeStruct((D_MODEL, D_FF), BF16),
                   jax.ShapeDtypeStruct((D_FF, D_MODEL), BF16), jax.ShapeDtypeStruct((D_MODEL, D_MODEL), BF16)],
        compiler_params=_params(("parallel", "parallel")),
        name=f"attn_dec_{l}",
    )(q, k, v, kc, vc, w_gate, w_up, w_down, w_out)


def _mix_block(pin_ref, cv_ref, wp_ref, ps_ref, cw_ref, o_ref, pad_ref, seq):
    n_seq = TS_MIX // seq
    zeros = jnp.zeros((n_seq, HALO, LANE), F32)
    pad_ref[:, :HALO, :] = zeros
    pad_ref[:, HALO + seq:, :] = zeros
    pos = lax.broadcasted_iota(jnp.int32, (seq, LANE), 0)

    def put(x):
        pad_ref[:, HALO:HALO + seq, :] = x.reshape(n_seq, seq, LANE)

    def at(j):
        return pad_ref[:, HALO + j:HALO + j + seq, :]

    for g, w in enumerate(POOL_WINDOWS):
        half = w // 2
        u = pin_ref[:, g * POOL_GC:(g + 1) * POOL_GC].astype(F32)
        put(u)
        total = functools.reduce(jnp.add, [at(j) for j in range(-half, half)])
        cnt = (jnp.minimum(pos + half, seq) - jnp.maximum(pos - half, 0)).astype(F32)
        d = (total / cnt).reshape(TS_MIX, LANE) - u
        y = jnp.dot(d.astype(BF16), wp_ref[g], preferred_element_type=F32)
        o_ref[:, g * POOL_GC:(g + 1) * POOL_GC] = (y * ps_ref[:, g * POOL_GC:(g + 1) * POOL_GC]).astype(BF16)

    for g in range(CONV_W // LANE):
        sl = slice(g * LANE, (g + 1) * LANE)
        ch = cv_ref[:, g * LANE:(g + 1) * LANE].astype(F32)
        cb = cv_ref[:, CONV_W + g * LANE:CONV_W + (g + 1) * LANE].astype(F32)
        cc = cv_ref[:, 2 * CONV_W + g * LANE:2 * CONV_W + (g + 1) * LANE].astype(F32)
        put(cc * ch)
        conv = at(-1) * cw_ref[0:1, sl] + at(0) * cw_ref[1:2, sl] + at(1) * cw_ref[2:3, sl]
        o_ref[:, POOL_W + g * LANE:POOL_W + (g + 1) * LANE] = (cb * conv.reshape(TS_MIX, LANE)).astype(BF16)


def _mix_kernel(pin_ref, cv_ref, wp_ref, ps_ref, cw_ref, o_ref, pad_ctx_ref, pad_dec_ref):
    i = pl.program_id(0)

    @pl.when(i < N_CTX // TS_MIX)
    def _():
        _mix_block(pin_ref, cv_ref, wp_ref, ps_ref, cw_ref, o_ref, pad_ctx_ref, SEQ)

    @pl.when(i >= N_CTX // TS_MIX)
    def _():
        _mix_block(pin_ref, cv_ref, wp_ref, ps_ref, cw_ref, o_ref, pad_dec_ref, DEC_SEQ)


def _mixers(l, pin, cv, W):
    return pl.pallas_call(
        _mix_kernel,
        grid=(N_TOK // TS_MIX,),
        in_specs=[
            pl.BlockSpec((TS_MIX, POOL_W), lambda i: (i, 0)),
            pl.BlockSpec((TS_MIX, 3 * CONV_W), lambda i: (i, 0)),
            pl.BlockSpec((None, len(POOL_WINDOWS), POOL_GC, POOL_GC), lambda i: (l, 0, 0, 0)),
            pl.BlockSpec((None, 1, POOL_W), lambda i: (l, 0, 0)),
            pl.BlockSpec((None, 3, CONV_W), lambda i: (l, 0, 0)),
        ],
        out_specs=pl.BlockSpec((TS_MIX, POOL_W + CONV_W), lambda i: (i, 0)),
        out_shape=jax.ShapeDtypeStruct((N_TOK, POOL_W + CONV_W), BF16),
        scratch_shapes=[pltpu.VMEM((TS_MIX // SEQ, SEQ + 2 * HALO, LANE), F32),
                        pltpu.VMEM((TS_MIX // DEC_SEQ, DEC_SEQ + 2 * HALO, LANE), F32)],
        compiler_params=_params(("parallel",), VMEM_LIMIT_BIG),
        name=f"mixers_{l}",
    )(pin, cv, W["w_pool"], W["pool_scale"], W["conv_w"])


def _out_kernel(*refs, n_x):
    x_refs, refs = refs[:n_x], refs[n_x:]
    actx_ref, adec_ref, pc_ref, wo_ref, g1_ref, sh_ref, sc_ref, gffn_ref, o_ref, h_ref = refs
    i = pl.program_id(0)
    att_w = N_HEADS * V_DIM

    def finish(att_ref):
        for r0 in range(0, TM_OUT, SUB_OUT):
            rows = slice(r0, r0 + SUB_OUT)
            y = jnp.dot(att_ref[rows, :], wo_ref[:att_w, :], preferred_element_type=F32)
            y = y + jnp.dot(pc_ref[rows, :], wo_ref[att_w:, :], preferred_element_type=F32)
            x = _token_rows(x_refs, rows, TM_OUT) + g1_ref[...] * y
            o_ref[rows, :] = x
            h = x * _rms(x, D_MODEL) * gffn_ref[...]
            h_ref[rows, :] = (h * (1.0 + sc_ref[...]) + sh_ref[...]).astype(BF16)

    @pl.when(i < N_CTX // TM_OUT)
    def _():
        finish(actx_ref)

    @pl.when(i >= N_CTX // TM_OUT)
    def _():
        finish(adec_ref)


def _out_proj(l, xs, att_ctx, att_dec, pc, w_out_b, mod4, W):
    tm = TM_OUT
    n_ctx_tiles = N_CTX // tm
    row = functools.partial(_mod_row, tm=tm)
    att_w = N_HEADS * V_DIM
    return pl.pallas_call(
        functools.partial(_out_kernel, n_x=len(xs)),
        grid=(N_TOK // tm,),
        in_specs=_token_specs(tm, D_MODEL, len(xs) == 2) + [
            pl.BlockSpec((tm, att_w), lambda i: (jnp.minimum(i, n_ctx_tiles - 1), 0)),
            pl.BlockSpec((tm, att_w), lambda i: (jnp.maximum(i - n_ctx_tiles, 0), 0)),
            pl.BlockSpec((tm, POOL_W + CONV_W), lambda i: (i, 0)),
            pl.BlockSpec((D_MODEL, D_MODEL), lambda i: (0, 0), pipeline_mode=pl.Buffered(1)),
            pl.BlockSpec((None, None, 1, D_MODEL), lambda i: (l, row(i), 0, 2)),
            pl.BlockSpec((None, None, 1, D_MODEL), lambda i: (l, row(i), 0, 3)),
            pl.BlockSpec((None, None, 1, D_MODEL), lambda i: (l, row(i), 0, 4)),
            pl.BlockSpec((None, 1, D_MODEL), lambda i: (l, 0, 0)),
        ],
        out_specs=[pl.BlockSpec((tm, D_MODEL), lambda i: (i, 0)), pl.BlockSpec((tm, D_MODEL), lambda i: (i, 0))],
        out_shape=[jax.ShapeDtypeStruct((N_TOK, D_MODEL), F32), jax.ShapeDtypeStruct((N_TOK, D_MODEL), BF16)],
        compiler_params=_params(("parallel",)),
        name=f"out_proj_{l}",
    )(*xs, att_ctx, att_dec, pc, w_out_b, mod4, mod4, mod4, W["g_ffn"])


def _ffn_kernel(x_ref, h_ref, g2_ref, wg_ref, wu_ref, wd_ref, o_ref):
    f = pl.program_id(1)

    @pl.when(f == 0)
    def _():
        o_ref[...] = x_ref[...]

    h = h_ref[...]
    y = None
    for c0 in range(0, TF_FFN, SUB_FFN):
        gate = jnp.dot(h, wg_ref[:, c0:c0 + SUB_FFN], preferred_element_type=F32)
        up = jnp.dot(h, wu_ref[:, c0:c0 + SUB_FFN], preferred_element_type=F32)
        part = jnp.dot((_silu(gate) * up).astype(BF16), wd_ref[c0:c0 + SUB_FFN, :], preferred_element_type=F32)
        y = part if y is None else y + part
    o_ref[...] += g2_ref[...] * y


def _ffn(l, x_all, h_all, mod4, ffn_w, row0=0, n_rows=N_TOK):
    tm, tf = TM_FFN, TF_FFN
    t0 = row0 // tm
    row = lambda i: _mod_row(i + t0, tm)
    return pl.pallas_call(
        _ffn_kernel,
        grid=(n_rows // tm, D_FF // tf),
        in_specs=[
            pl.BlockSpec((tm, D_MODEL), lambda i, f: (i + t0, 0)),
            pl.BlockSpec((tm, D_MODEL), lambda i, f: (i + t0, 0)),
            pl.BlockSpec((None, None, 1, D_MODEL), lambda i, f: (l, row(i), 0, 5)),
            pl.BlockSpec((D_MODEL, tf), lambda i, f: (0, f)),
            pl.BlockSpec((D_MODEL, tf), lambda i, f: (0, f)),
            pl.BlockSpec((tf, D_MODEL), lambda i, f: (f, 0)),
        ],
        out_specs=pl.BlockSpec((tm, D_MODEL), lambda i, f: (i, 0)),
        out_shape=jax.ShapeDtypeStruct((n_rows, D_MODEL), F32),
        compiler_params=_params(("parallel", "arbitrary"), VMEM_LIMIT_BIG),
        name=f"ffn_{l}_{row0}",
    )(x_all, h_all, mod4, *ffn_w)


def _prep_weights(g_mix, g_ffn, w_in, g_q_a, w_uq, g_kv_a, w_ukv, g_qn, g_kn, w_pool, pool_scale, conv_w):
    w_in_a, w_in_b = _prep_w_in(w_in)
    w_uq_p = jnp.pad(w_uq.reshape(DEPTH, Q_LORA, N_HEADS, QK_DIM), ((0, 0), (0, 0), (0, 0), (0, HEAD_PAD - QK_DIM)))
    head_vec = lambda g: jnp.pad(g, ((0, 0), (0, HEAD_PAD - QK_DIM))).reshape(DEPTH, 1, HEAD_PAD)
    cos, sin = _rope_tables()
    return dict(
        g_mix=g_mix.reshape(DEPTH, 1, D_MODEL), g_ffn=g_ffn.reshape(DEPTH, 1, D_MODEL),
        w_in_a=w_in_a, w_in_b=w_in_b, g_q_a=g_q_a.reshape(DEPTH, 1, Q_LORA),
        w_uq=w_uq_p.reshape(DEPTH, Q_LORA, N_HEADS * HEAD_PAD).astype(BF16),
        g_kv_a=g_kv_a.reshape(DEPTH, 1, KV_LORA), w_ukv=w_ukv.astype(BF16),
        g_qn=head_vec(g_qn), g_kn=head_vec(g_kn), w_pool=w_pool.astype(BF16),
        pool_scale=pool_scale.reshape(DEPTH, 1, POOL_W), conv_w=conv_w, cos=cos, sin=sin)


def kernel(x_prompt, x_sample, cache_ckv, cache_krope, c, c_ctx, w_ada, b_ada, g_mix, g_ffn, w_in, g_q_a, w_uq,
           g_kv_a, w_ukv, g_qn, g_kn, w_pool, pool_scale, conv_w, w_out, w_gate, w_up, w_down):
    W = _prep_weights(g_mix, g_ffn, w_in, g_q_a, w_uq, g_kv_a, w_ukv, g_qn, g_kn, w_pool, pool_scale, conv_w)
    m_all = jnp.concatenate([c_ctx[None, :], c, jnp.zeros((MOD_ROWS - 1 - DEC_BATCH, D_MODEL), F32)], axis=0)
    mod4 = _ada(m_all, w_ada, b_ada).reshape(DEPTH, MOD_ROWS, 1, N_MOD)
    cache_kr_pad = jnp.pad(cache_krope, ((0, 0), (0, 0), (0, 0), (0, LANE - ROPE_DIM)))
    kc, vc = _cache_kv(cache_ckv, cache_kr_pad, W)

    xs = (x_prompt.reshape(N_CTX, D_MODEL), x_sample.reshape(N_DEC, D_MODEL))
    new_ckv, new_kr = [], []
    for l in range(DEPTH):
        q, k, v, ckvn, kr, pin, cv = _in_proj(l, xs, mod4, W)
        att_ctx = _attn_ctx(l, q, k, v)
        att_dec, *ffn_w, w_out_b = _attn_dec(l, q, k, v, kc, vc, w_gate, w_up, w_down, w_out)
        pc = _mixers(l, pin, cv, W)
        x_all, h_all = _out_proj(l, xs, att_ctx, att_dec, pc, w_out_b, mod4, W)
        if l < DEPTH - 1:
            xs = (_ffn(l, x_all, h_all, mod4, ffn_w),)
        else:
            y_ctx = _ffn(l, x_all, h_all, mod4, ffn_w, 0, N_CTX)
            y_dec = _ffn(l, x_all, h_all, mod4, ffn_w, N_CTX, N_DEC)
        new_ckv.append(ckvn[:N_CTX].reshape(BATCH, SEQ, KV_LORA))
        new_kr.append(kr[:N_CTX].reshape(BATCH, SEQ, ROPE_DIM))
    y_prompt = y_ctx.reshape(BATCH, SEQ, D_MODEL)
    y_sample = y_dec.reshape(DEC_BATCH, DEC_SEQ, D_MODEL)
    return y_prompt, y_sample, jnp.stack(new_ckv, axis=1), jnp.stack(new_kr, axis=1)
```

```python
import functools

import jax
import jax.numpy as jnp
import numpy as np
from jax import lax
from jax.experimental import pallas as pl
from jax.experimental.pallas import tpu as pltpu

D_MODEL = 2048
BATCH = 16
SEQ = 256
DEPTH = 4
DEC_BATCH = 4
DEC_SEQ = 2048
PAST_LEN = 512
GRID_W = 64
N_HEADS = 8
QK_NOPE = 128
ROPE_DIM = 64
QK_DIM = QK_NOPE + ROPE_DIM
V_DIM = 128
Q_LORA = 512
KV_LORA = 256
POOL_WINDOWS = (2, 4, 8, 16)
POOL_GC = 128
POOL_W = 512
CONV_W = 512
D_FF = 5632
ROPE_BASE = 10000.0
EPS = 1e-6

N_CTX = BATCH * SEQ
N_DEC = DEC_BATCH * DEC_SEQ
N_TOK = N_CTX + N_DEC
N_MOD = 6 * D_MODEL
MOD_ROWS = 8

LANE = 128
HEAD_PAD = 2 * LANE
IN_W = Q_LORA + KV_LORA + ROPE_DIM + POOL_W + 3 * CONV_W
IN_CQ = 0
IN_CKV = IN_CQ + Q_LORA
IN_KR = IN_CKV + KV_LORA
IN_A = IN_KR + LANE
IN_A_SRC = IN_KR + ROPE_DIM
IN_B = POOL_W + 3 * CONV_W

TM_IN = 512
SUB_IN = 256
TM_OUT = 512
SUB_OUT = 256
TM_FFN = 1024
TF_FFN = 512
SUB_FFN = 256
TN_ADA = 2048
NB_CTX = 2
TS_MIX = 2048
HALO = 8
TQ_DEC = 256
TR_WIN = 256
VMEM_LIMIT = 56 * 1024 * 1024
VMEM_LIMIT_BIG = 62 * 1024 * 1024

F32 = jnp.float32
BF16 = jnp.bfloat16
Q_SCALE = QK_DIM ** -0.5 * 1.4426950408889634


def _mod_row(i, tm):
    n_ctx_tiles = N_CTX // tm
    return jnp.where(i < n_ctx_tiles, 0, 1 + (i - n_ctx_tiles) // (DEC_SEQ // tm))


def _params(sem, vmem=VMEM_LIMIT):
    return pltpu.CompilerParams(dimension_semantics=sem, vmem_limit_bytes=vmem)


def _silu(x):
    return x / (1.0 + jnp.exp(-x))


def _rms(x, n):
    return lax.rsqrt(jnp.sum(x * x, axis=-1, keepdims=True) * (1.0 / n) + EPS)


def _token_specs(tm, width, split):
    if not split:
        return [pl.BlockSpec((tm, width), lambda i, *_: (i, 0))]
    n_ctx_tiles = N_CTX // tm
    return [pl.BlockSpec((tm, width), lambda i, *_: (jnp.minimum(i, n_ctx_tiles - 1), 0)),
            pl.BlockSpec((tm, width), lambda i, *_: (jnp.maximum(i - n_ctx_tiles, 0), 0))]


def _token_rows(x_refs, rows, tm):
    if len(x_refs) == 1:
        return x_refs[0][rows, :]
    return jnp.where(pl.program_id(0) < N_CTX // tm, x_refs[0][rows, :], x_refs[1][rows, :])


def _ada_kernel(m_ref, w_ref, b_ref, o_ref):
    a = _silu(m_ref[...]).astype(BF16)
    o_ref[...] = jnp.dot(a, w_ref[...].astype(BF16), preferred_element_type=F32) + b_ref[...]


def _ada(m_all, w_ada, b_ada):
    return pl.pallas_call(
        _ada_kernel,
        grid=(DEPTH, N_MOD // TN_ADA),
        in_specs=[
            pl.BlockSpec((MOD_ROWS, D_MODEL), lambda l, j: (0, 0)),
            pl.BlockSpec((None, D_MODEL, TN_ADA), lambda l, j: (l, 0, j)),
            pl.BlockSpec((None, 1, TN_ADA), lambda l, j: (l, 0, j)),
        ],
        out_specs=pl.BlockSpec((None, MOD_ROWS, TN_ADA), lambda l, j: (l, 0, j)),
        out_shape=jax.ShapeDtypeStruct((DEPTH, MOD_ROWS, N_MOD), F32),
        compiler_params=_params(("parallel", "parallel")),
        name="ada_mod",
    )(m_all, w_ada, b_ada.reshape(DEPTH, 1, N_MOD))


def _w_in_kernel(wt_ref, a_ref, b_ref):
    a = wt_ref[:IN_A, :].T
    lane = lax.broadcasted_iota(jnp.int32, a.shape, 1)
    a_ref[...] = jnp.where(lane < IN_A_SRC, a, 0.0).astype(BF16)
    b_ref[...] = wt_ref[IN_A_SRC:, :].T.astype(BF16)


def _prep_w_in(w_in):
    return pl.pallas_call(
        _w_in_kernel,
        grid=(DEPTH, D_MODEL // TR_WIN),
        in_specs=[pl.BlockSpec((None, IN_W, TR_WIN), lambda l, r: (l, 0, r))],
        out_specs=[pl.BlockSpec((None, TR_WIN, IN_A), lambda l, r: (l, r, 0)),
                   pl.BlockSpec((None, TR_WIN, IN_B), lambda l, r: (l, r, 0))],
        out_shape=[jax.ShapeDtypeStruct((DEPTH, D_MODEL, IN_A), BF16),
                   jax.ShapeDtypeStruct((DEPTH, D_MODEL, IN_B), BF16)],
        compiler_params=_params(("parallel", "parallel")),
        name="prep_w_in",
    )(jnp.swapaxes(w_in, 1, 2))


def _rope_tables():
    f32 = np.float32
    t = np.arange(DEC_SEQ)
    r_pos = (t // GRID_W).astype(f32)
    c_pos = (t % GRID_W).astype(f32)
    nf = ROPE_DIM // 4
    inv = f32(ROPE_BASE) ** (-np.arange(nf, dtype=f32) / f32(nf))
    ang_r = r_pos[:, None] * inv[None, :]
    ang_c = c_pos[:, None] * inv[None, :]
    zeros = np.zeros((DEC_SEQ, LANE - ROPE_DIM), f32)
    cos = np.concatenate([np.cos(ang_r), np.cos(ang_r), np.cos(ang_c), np.cos(ang_c), zeros], axis=1)
    sin = np.concatenate([-np.sin(ang_r), np.sin(ang_r), -np.sin(ang_c), np.sin(ang_c), zeros], axis=1)
    ident_cos = np.concatenate([np.ones((TM_IN, ROPE_DIM), f32), np.zeros((TM_IN, LANE - ROPE_DIM), f32)], axis=1)
    ident_sin = np.zeros((TM_IN, LANE), f32)
    return (jnp.asarray(np.concatenate([ident_cos, cos], axis=0).astype(f32)),
            jnp.asarray(np.concatenate([ident_sin, sin], axis=0).astype(f32)))


def _rope(x, cos, sin, first_half):
    n = x.shape[-1]
    swapped = jnp.where(first_half, pltpu.roll(x, n - ROPE_DIM // 4, 1), pltpu.roll(x, ROPE_DIM // 4, 1))
    return x * cos + swapped * sin


def _in_kernel(*refs, n_x):
    x_refs, refs = refs[:n_x], refs[n_x:]
    (sh_ref, sc_ref, gmix_ref, wa_ref, wb_ref, gqa_ref, wuq_ref, gqn_ref, gkva_ref, wukv_ref, gkn_ref,
     cos_ref, sin_ref, q_ref, k_ref, v_ref, ckv_ref, kr_ref, pin_ref, cv_ref, hb_scr) = refs
    lane = lax.broadcasted_iota(jnp.int32, (SUB_IN, LANE), 1)
    first_half = (lane & (ROPE_DIM // 4)) == 0
    gqn = gqn_ref[...]
    gkn = gkn_ref[...]

    for r0 in range(0, TM_IN, SUB_IN):
        rows = slice(r0, r0 + SUB_IN)
        x = _token_rows(x_refs, rows, TM_IN)
        h = x * _rms(x, D_MODEL) * gmix_ref[...]
        h = h * (1.0 + sc_ref[...]) + sh_ref[...]
        hb = h.astype(BF16)
        hb_scr[rows, :] = hb
        u = jnp.dot(hb, wa_ref[...], preferred_element_type=F32)
        cos = cos_ref[rows, :]
        sin = sin_ref[rows, :]

        cq = u[:, IN_CQ:IN_CQ + Q_LORA]
        cqn = cq * _rms(cq, Q_LORA) * gqa_ref[...]
        q = jnp.dot(cqn.astype(BF16), wuq_ref[...], preferred_element_type=F32)
        for hd in range(N_HEADS):
            qh = q[:, hd * HEAD_PAD:(hd + 1) * HEAD_PAD]
            qh = qh * (_rms(qh, QK_DIM) * Q_SCALE) * gqn
            q_ref[rows, hd * HEAD_PAD:hd * HEAD_PAD + LANE] = qh[:, :LANE].astype(BF16)
            q_ref[rows, hd * HEAD_PAD + LANE:(hd + 1) * HEAD_PAD] = (
                _rope(qh[:, LANE:], cos, sin, first_half).astype(BF16))

        ckv = u[:, IN_CKV:IN_CKV + KV_LORA]
        ckvn = ckv * _rms(ckv, KV_LORA) * gkva_ref[...]
        ckv_ref[rows, :] = ckvn
        kv = jnp.dot(ckvn.astype(BF16), wukv_ref[...], preferred_element_type=F32)
        krb = u[:, IN_KR:IN_KR + LANE]
        kr_ref[rows, :] = krb[:, :ROPE_DIM]
        kr_ss = jnp.sum(krb * krb, axis=-1, keepdims=True)
        krg = _rope(krb * gkn[:, LANE:], cos, sin, first_half)
        for hd in range(N_HEADS):
            kn = kv[:, hd * HEAD_PAD:hd * HEAD_PAD + LANE]
            rs = lax.rsqrt((jnp.sum(kn * kn, axis=-1, keepdims=True) + kr_ss) * (1.0 / QK_DIM) + EPS)
            k_ref[rows, hd * HEAD_PAD:hd * HEAD_PAD + LANE] = (kn * rs * gkn[:, :LANE]).astype(BF16)
            k_ref[rows, hd * HEAD_PAD + LANE:(hd + 1) * HEAD_PAD] = (krg * rs).astype(BF16)
            v_ref[rows, hd * V_DIM:(hd + 1) * V_DIM] = kv[:, hd * HEAD_PAD + LANE:(hd + 1) * HEAD_PAD].astype(BF16)

    um = jnp.dot(hb_scr[...], wb_ref[...], preferred_element_type=F32)
    pin_ref[...] = um[:, :POOL_W].astype(BF16)
    cv_ref[...] = um[:, POOL_W:].astype(BF16)


def _in_proj(l, xs, mod4, W):
    tm = TM_IN
    n_ctx_tiles = N_CTX // tm
    row = functools.partial(_mod_row, tm=tm)
    vec = lambda w: pl.BlockSpec((None, 1, w), lambda i: (l, 0, 0))
    mat = lambda r, c: pl.BlockSpec((None, r, c), lambda i: (l, 0, 0), pipeline_mode=pl.Buffered(1))
    tab = pl.BlockSpec((tm, LANE), lambda i: (jnp.where(i < n_ctx_tiles, 0, 1 + (i - n_ctx_tiles) % (DEC_SEQ // tm)), 0))
    tok = lambda w: pl.BlockSpec((tm, w), lambda i: (i, 0))
    return pl.pallas_call(
        functools.partial(_in_kernel, n_x=len(xs)),
        grid=(N_TOK // tm,),
        in_specs=_token_specs(tm, D_MODEL, len(xs) == 2) + [
            pl.BlockSpec((None, None, 1, D_MODEL), lambda i: (l, row(i), 0, 0)),
            pl.BlockSpec((None, None, 1, D_MODEL), lambda i: (l, row(i), 0, 1)),
            vec(D_MODEL), mat(D_MODEL, IN_A), mat(D_MODEL, IN_B), vec(Q_LORA), mat(Q_LORA, N_HEADS * HEAD_PAD),
            vec(HEAD_PAD),
            vec(KV_LORA), mat(KV_LORA, N_HEADS * HEAD_PAD), vec(HEAD_PAD), tab, tab,
        ],
        out_specs=[tok(N_HEADS * HEAD_PAD), tok(N_HEADS * HEAD_PAD), tok(N_HEADS * V_DIM), tok(KV_LORA),
                   tok(ROPE_DIM), tok(POOL_W), tok(3 * CONV_W)],
        out_shape=[
            jax.ShapeDtypeStruct((N_TOK, N_HEADS * HEAD_PAD), BF16),
            jax.ShapeDtypeStruct((N_TOK, N_HEADS * HEAD_PAD), BF16),
            jax.ShapeDtypeStruct((N_TOK, N_HEADS * V_DIM), BF16),
            jax.ShapeDtypeStruct((N_TOK, KV_LORA), F32),
            jax.ShapeDtypeStruct((N_TOK, ROPE_DIM), F32),
            jax.ShapeDtypeStruct((N_TOK, POOL_W), BF16),
            jax.ShapeDtypeStruct((N_TOK, 3 * CONV_W), BF16),
        ],
        scratch_shapes=[pltpu.VMEM((tm, D_MODEL), BF16)],
        compiler_params=_params(("parallel",), VMEM_LIMIT_BIG if len(xs) == 2 else VMEM_LIMIT),
        name=f"in_proj_{l}",
    )(*xs, mod4, mod4, W["g_mix"], W["w_in_a"], W["w_in_b"], W["g_q_a"], W["w_uq"], W["g_qn"], W["g_kv_a"],
      W["w_ukv"], W["g_kn"], W["cos"], W["sin"])


def _kvc_kernel(ckv_ref, kr_ref, wukv_ref, gkn_ref, k_ref, v_ref):
    kv = jnp.dot(ckv_ref[...].astype(BF16), wukv_ref[...], preferred_element_type=F32)
    krb = kr_ref[...]
    kr_ss = jnp.sum(krb * krb, axis=-1, keepdims=True)
    gkn = gkn_ref[...]
    krg = krb * gkn[:, LANE:]
    for hd in range(N_HEADS):
        kn = kv[:, hd * HEAD_PAD:hd * HEAD_PAD + LANE]
        rs = lax.rsqrt((jnp.sum(kn * kn, axis=-1, keepdims=True) + kr_ss) * (1.0 / QK_DIM) + EPS)
        k_ref[:, hd * HEAD_PAD:hd * HEAD_PAD + LANE] = (kn * rs * gkn[:, :LANE]).astype(BF16)
        k_ref[:, hd * HEAD_PAD + LANE:(hd + 1) * HEAD_PAD] = (krg * rs).astype(BF16)
        v_ref[:, hd * V_DIM:(hd + 1) * V_DIM] = kv[:, hd * HEAD_PAD + LANE:(hd + 1) * HEAD_PAD].astype(BF16)


def _cache_kv(cache_ckv, cache_kr_pad, W):
    return pl.pallas_call(
        _kvc_kernel,
        grid=(DEPTH, DEC_BATCH),
        in_specs=[
            pl.BlockSpec((None, None, PAST_LEN, KV_LORA), lambda l, b: (b, l, 0, 0)),
            pl.BlockSpec((None, None, PAST_LEN, LANE), lambda l, b: (b, l, 0, 0)),
            pl.BlockSpec((None, KV_LORA, N_HEADS * HEAD_PAD), lambda l, b: (l, 0, 0)),
            pl.BlockSpec((None, 1, HEAD_PAD), lambda l, b: (l, 0, 0)),
        ],
        out_specs=[
            pl.BlockSpec((None, None, PAST_LEN, N_HEADS * HEAD_PAD), lambda l, b: (l, b, 0, 0)),
            pl.BlockSpec((None, None, PAST_LEN, N_HEADS * V_DIM), lambda l, b: (l, b, 0, 0)),
        ],
        out_shape=[
            jax.ShapeDtypeStruct((DEPTH, DEC_BATCH, PAST_LEN, N_HEADS * HEAD_PAD), BF16),
            jax.ShapeDtypeStruct((DEPTH, DEC_BATCH, PAST_LEN, N_HEADS * V_DIM), BF16),
        ],
        compiler_params=_params(("parallel", "parallel")),
        name="cache_kv",
    )(cache_ckv, cache_kr_pad, W["w_ukv"], W["g_kn"])


def _softmax_pv(q, ks, vs):
    ss = [lax.dot_general(q, k, (((1,), (1,)), ((), ())), preferred_element_type=F32) for k in ks]
    m = functools.reduce(jnp.maximum, [jnp.max(s, axis=-1, keepdims=True) for s in ss])
    ps = [jnp.exp2(s - m) for s in ss]
    den = functools.reduce(jnp.add, [jnp.sum(p, axis=-1, keepdims=True) for p in ps])
    o = functools.reduce(jnp.add, [jnp.dot(p.astype(BF16), v, preferred_element_type=F32) for p, v in zip(ps, vs)])
    return o / den


def _attn_ctx_kernel(q_ref, k_ref, v_ref, o_ref):
    for r0 in range(0, NB_CTX * SEQ, SEQ):
        rows = slice(r0, r0 + SEQ)
        for hd in range(N_HEADS):
            q = q_ref[rows, hd * HEAD_PAD:(hd + 1) * HEAD_PAD]
            k = k_ref[rows, hd * HEAD_PAD:(hd + 1) * HEAD_PAD]
            v = v_ref[rows, hd * V_DIM:(hd + 1) * V_DIM]
            o_ref[rows, hd * V_DIM:(hd + 1) * V_DIM] = _softmax_pv(q, [k], [v]).astype(BF16)


def _attn_ctx(l, q, k, v):
    rows = NB_CTX * SEQ
    return pl.pallas_call(
        _attn_ctx_kernel,
        grid=(BATCH // NB_CTX,),
        in_specs=[
            pl.BlockSpec((rows, N_HEADS * HEAD_PAD), lambda b: (b, 0)),
            pl.BlockSpec((rows, N_HEADS * HEAD_PAD), lambda b: (b, 0)),
            pl.BlockSpec((rows, N_HEADS * V_DIM), lambda b: (b, 0)),
        ],
        out_specs=pl.BlockSpec((rows, N_HEADS * V_DIM), lambda b: (b, 0)),
        out_shape=jax.ShapeDtypeStruct((N_CTX, N_HEADS * V_DIM), BF16),
        compiler_params=_params(("parallel",)),
        name=f"attn_ctx_{l}",
    )(q, k, v)


def _attn_dec_kernel(q_ref, k_ref, v_ref, kc_ref, vc_ref, wg_ref, wu_ref, wd_ref, wo_ref,
                     o_ref, wgb_ref, wub_ref, wdb_ref, wob_ref):
    ks = [k_ref[...], kc_ref[...]]
    vs = [v_ref[...], vc_ref[...]]
    for r in range(0, DEC_SEQ, TQ_DEC):
        o_ref[r:r + TQ_DEC, :] = _softmax_pv(q_ref[r:r + TQ_DEC, :], ks, vs).astype(BF16)
    for t in range(D_FF // TF_FFN):
        wgb_ref[t] = wg_ref[:, t * TF_FFN:(t + 1) * TF_FFN].astype(BF16)
        wub_ref[t] = wu_ref[:, t * TF_FFN:(t + 1) * TF_FFN].astype(BF16)
    wdb_ref[...] = wd_ref[...].astype(BF16)
    wob_ref[...] = wo_ref[...].astype(BF16)


def _attn_dec(l, q, k, v, kc, vc, w_gate, w_up, w_down, w_out):
    first = N_CTX // DEC_SEQ
    steps = DEC_BATCH * N_HEADS
    slab = lambda rows, cols: pl.BlockSpec((None, rows // steps, cols), lambda b, h: (l, b * N_HEADS + h, 0))
    slab_out = lambda rows, cols: pl.BlockSpec((rows // steps, cols), lambda b, h: (b * N_HEADS + h, 0))
    n_f = D_FF // TF_FFN
    tiled_out = pl.BlockSpec((n_f, D_MODEL // steps, TF_FFN), lambda b, h: (0, b * N_HEADS + h, 0))
    tiled_shape = jax.ShapeDtypeStruct((n_f, D_MODEL, TF_FFN), BF16)
    return pl.pallas_call(
        _attn_dec_kernel,
        grid=(DEC_BATCH, N_HEADS),
        in_specs=[
            pl.BlockSpec((DEC_SEQ, HEAD_PAD), lambda b, h: (first + b, h)),
            pl.BlockSpec((DEC_SEQ, HEAD_PAD), lambda b, h: (first + b, h)),
            pl.BlockSpec((DEC_SEQ, V_DIM), lambda b, h: (first + b, h)),
            pl.BlockSpec((None, None, PAST_LEN, HEAD_PAD), lambda b, h: (l, b, 0, h)),
            pl.BlockSpec((None, None, PAST_LEN, V_DIM), lambda b, h: (l, b, 0, h)),
            slab(D_MODEL, D_FF), slab(D_MODEL, D_FF), slab(D_FF, D_MODEL), slab(D_MODEL, D_MODEL),
        ],
        out_specs=[pl.BlockSpec((DEC_SEQ, V_DIM), lambda b, h: (b, h)),
                   tiled_out, tiled_out, slab_out(D_FF, D_MODEL), slab_out(D_MODEL, D_MODEL)],
        out_shape=[jax.ShapeDtypeStruct((N_DEC, N_HEADS * V_DIM), BF16), tiled_shape, tiled_shape,
                   jax.ShapeDtypeStruct((D_FF, D_MODEL), BF16), jax.ShapeDtypeStruct((D_MODEL, D_MODEL), BF16)],
        compiler_params=_params(("parallel", "parallel")),
        name=f"attn_dec_{l}",
    )(q, k, v, kc, vc, w_gate, w_up, w_down, w_out)


def _mix_block(pin_ref, cv_ref, wp_ref, ps_ref, cw_ref, o_ref, pad_ref, seq):
    n_seq = TS_MIX // seq
    zeros = jnp.zeros((n_seq, HALO, LANE), F32)
    pad_ref[:, :HALO, :] = zeros
    pad_ref[:, HALO + seq:, :] = zeros
    pos = lax.broadcasted_iota(jnp.int32, (seq, LANE), 0)

    def put(x):
        pad_ref[:, HALO:HALO + seq, :] = x.reshape(n_seq, seq, LANE)

    def at(j):
        return pad_ref[:, HALO + j:HALO + j + seq, :]

    for g, w in enumerate(POOL_WINDOWS):
        half = w // 2
        u = pin_ref[:, g * POOL_GC:(g + 1) * POOL_GC].astype(F32)
        put(u)
        total = functools.reduce(jnp.add, [at(j) for j in range(-half, half)])
        cnt = (jnp.minimum(pos + half, seq) - jnp.maximum(pos - half, 0)).astype(F32)
        d = (total / cnt).reshape(TS_MIX, LANE) - u
        y = jnp.dot(d.astype(BF16), wp_ref[g], preferred_element_type=F32)
        o_ref[:, g * POOL_GC:(g + 1) * POOL_GC] = (y * ps_ref[:, g * POOL_GC:(g + 1) * POOL_GC]).astype(BF16)

    for g in range(CONV_W // LANE):
        sl = slice(g * LANE, (g + 1) * LANE)
        ch = cv_ref[:, g * LANE:(g + 1) * LANE].astype(F32)
        cb = cv_ref[:, CONV_W + g * LANE:CONV_W + (g + 1) * LANE].astype(F32)
        cc = cv_ref[:, 2 * CONV_W + g * LANE:2 * CONV_W + (g + 1) * LANE].astype(F32)
        put(cc * ch)
        conv = at(-1) * cw_ref[0:1, sl] + at(0) * cw_ref[1:2, sl] + at(1) * cw_ref[2:3, sl]
        o_ref[:, POOL_W + g * LANE:POOL_W + (g + 1) * LANE] = (cb * conv.reshape(TS_MIX, LANE)).astype(BF16)


def _mix_kernel(pin_ref, cv_ref, wp_ref, ps_ref, cw_ref, o_ref, pad_ctx_ref, pad_dec_ref):
    i = pl.program_id(0)

    @pl.when(i < N_CTX // TS_MIX)
    def _():
        _mix_block(pin_ref, cv_ref, wp_ref, ps_ref, cw_ref, o_ref, pad_ctx_ref, SEQ)

    @pl.when(i >= N_CTX // TS_MIX)
    def _():
        _mix_block(pin_ref, cv_ref, wp_ref, ps_ref, cw_ref, o_ref, pad_dec_ref, DEC_SEQ)


def _mixers(l, pin, cv, W):
    return pl.pallas_call(
        _mix_kernel,
        grid=(N_TOK // TS_MIX,),
        in_specs=[
            pl.BlockSpec((TS_MIX, POOL_W), lambda i: (i, 0)),
            pl.BlockSpec((TS_MIX, 3 * CONV_W), lambda i: (i, 0)),
            pl.BlockSpec((None, len(POOL_WINDOWS), POOL_GC, POOL_GC), lambda i: (l, 0, 0, 0)),
            pl.BlockSpec((None, 1, POOL_W), lambda i: (l, 0, 0)),
            pl.BlockSpec((None, 3, CONV_W), lambda i: (l, 0, 0)),
        ],
        out_specs=pl.BlockSpec((TS_MIX, POOL_W + CONV_W), lambda i: (i, 0)),
        out_shape=jax.ShapeDtypeStruct((N_TOK, POOL_W + CONV_W), BF16),
        scratch_shapes=[pltpu.VMEM((TS_MIX // SEQ, SEQ + 2 * HALO, LANE), F32),
                        pltpu.VMEM((TS_MIX // DEC_SEQ, DEC_SEQ + 2 * HALO, LANE), F32)],
        compiler_params=_params(("parallel",), VMEM_LIMIT_BIG),
        name=f"mixers_{l}",
    )(pin, cv, W["w_pool"], W["pool_scale"], W["conv_w"])


def _out_kernel(*refs, n_x):
    x_refs, refs = refs[:n_x], refs[n_x:]
    actx_ref, adec_ref, pc_ref, wo_ref, g1_ref, sh_ref, sc_ref, gffn_ref, o_ref, h_ref = refs
    i = pl.program_id(0)
    att_w = N_HEADS * V_DIM

    def finish(att_ref):
        for r0 in range(0, TM_OUT, SUB_OUT):
            rows = slice(r0, r0 + SUB_OUT)
            y = jnp.dot(att_ref[rows, :], wo_ref[:att_w, :], preferred_element_type=F32)
            y = y + jnp.dot(pc_ref[rows, :], wo_ref[att_w:, :], preferred_element_type=F32)
            x = _token_rows(x_refs, rows, TM_OUT) + g1_ref[...] * y
            o_ref[rows, :] = x
            h = x * _rms(x, D_MODEL) * gffn_ref[...]
            h_ref[rows, :] = (h * (1.0 + sc_ref[...]) + sh_ref[...]).astype(BF16)

    @pl.when(i < N_CTX // TM_OUT)
    def _():
        finish(actx_ref)

    @pl.when(i >= N_CTX // TM_OUT)
    def _():
        finish(adec_ref)


def _out_proj(l, xs, att_ctx, att_dec, pc, w_out_b, mod4, W):
    tm = TM_OUT
    n_ctx_tiles = N_CTX // tm
    row = functools.partial(_mod_row, tm=tm)
    att_w = N_HEADS * V_DIM
    return pl.pallas_call(
        functools.partial(_out_kernel, n_x=len(xs)),
        grid=(N_TOK // tm,),
        in_specs=_token_specs(tm, D_MODEL, len(xs) == 2) + [
            pl.BlockSpec((tm, att_w), lambda i: (jnp.minimum(i, n_ctx_tiles - 1), 0)),
            pl.BlockSpec((tm, att_w), lambda i: (jnp.maximum(i - n_ctx_tiles, 0), 0)),
            pl.BlockSpec((tm, POOL_W + CONV_W), lambda i: (i, 0)),
            pl.BlockSpec((D_MODEL, D_MODEL), lambda i: (0, 0), pipeline_mode=pl.Buffered(1)),
            pl.BlockSpec((None, None, 1, D_MODEL), lambda i: (l, row(i), 0, 2)),
            pl.BlockSpec((None, None, 1, D_MODEL), lambda i: (l, row(i), 0, 3)),
            pl.BlockSpec((None, None, 1, D_MODEL), lambda i: (l, row(i), 0, 4)),
            pl.BlockSpec((None, 1, D_MODEL), lambda i: (l, 0, 0)),
        ],
        out_specs=[pl.BlockSpec((tm, D_MODEL), lambda i: (i, 0)), pl.BlockSpec((tm, D_MODEL), lambda i: (i, 0))],
        out_shape=[jax.ShapeDtypeStruct((N_TOK, D_MODEL), F32), jax.ShapeDtypeStruct((N_TOK, D_MODEL), BF16)],
        compiler_params=_params(("parallel",)),
        name=f"out_proj_{l}",
    )(*xs, att_ctx, att_dec, pc, w_out_b, mod4, mod4, mod4, W["g_ffn"])


def _ffn_kernel(x_ref, h_ref, g2_ref, wg_ref, wu_ref, wd_ref, o_ref):
    f = pl.program_id(1)

    @pl.when(f == 0)
    def _():
        o_ref[...] = x_ref[...]

    h = h_ref[...]
    y = None
    for c0 in range(0, TF_FFN, SUB_FFN):
        gate = jnp.dot(h, wg_ref[:, c0:c0 + SUB_FFN], preferred_element_type=F32)
        up = jnp.dot(h, wu_ref[:, c0:c0 + SUB_FFN], preferred_element_type=F32)
        part = jnp.dot((_silu(gate) * up).astype(BF16), wd_ref[c0:c0 + SUB_FFN, :], preferred_element_type=F32)
        y = part if y is None else y + part
    o_ref[...] += g2_ref[...] * y


def _ffn(l, x_all, h_all, mod4, ffn_w, row0=0, n_rows=N_TOK):
    tm, tf = TM_FFN, TF_FFN
    t0 = row0 // tm
    row = lambda i: _mod_row(i + t0, tm)
    return pl.pallas_call(
        _ffn_kernel,
        grid=(n_rows // tm, D_FF // tf),
        in_specs=[
            pl.BlockSpec((tm, D_MODEL), lambda i, f: (i + t0, 0)),
            pl.BlockSpec((tm, D_MODEL), lambda i, f: (i + t0, 0)),
            pl.BlockSpec((None, None, 1, D_MODEL), lambda i, f: (l, row(i), 0, 5)),
            pl.BlockSpec((None, D_MODEL, tf), lambda i, f: (f, 0, 0)),
            pl.BlockSpec((None, D_MODEL, tf), lambda i, f: (f, 0, 0)),
            pl.BlockSpec((tf, D_MODEL), lambda i, f: (f, 0)),
        ],
        out_specs=pl.BlockSpec((tm, D_MODEL), lambda i, f: (i, 0)),
        out_shape=jax.ShapeDtypeStruct((n_rows, D_MODEL), F32),
        compiler_params=_params(("parallel", "arbitrary"), VMEM_LIMIT_BIG),
        name=f"ffn_{l}_{row0}",
    )(x_all, h_all, mod4, *ffn_w)


def _prep_weights(g_mix, g_ffn, w_in, g_q_a, w_uq, g_kv_a, w_ukv, g_qn, g_kn, w_pool, pool_scale, conv_w):
    w_in_a, w_in_b = _prep_w_in(w_in)
    w_uq_p = jnp.pad(w_uq.reshape(DEPTH, Q_LORA, N_HEADS, QK_DIM), ((0, 0), (0, 0), (0, 0), (0, HEAD_PAD - QK_DIM)))
    head_vec = lambda g: jnp.pad(g, ((0, 0), (0, HEAD_PAD - QK_DIM))).reshape(DEPTH, 1, HEAD_PAD)
    cos, sin = _rope_tables()
    return dict(
        g_mix=g_mix.reshape(DEPTH, 1, D_MODEL), g_ffn=g_ffn.reshape(DEPTH, 1, D_MODEL),
        w_in_a=w_in_a, w_in_b=w_in_b, g_q_a=g_q_a.reshape(DEPTH, 1, Q_LORA),
        w_uq=w_uq_p.reshape(DEPTH, Q_LORA, N_HEADS * HEAD_PAD).astype(BF16),
        g_kv_a=g_kv_a.reshape(DEPTH, 1, KV_LORA), w_ukv=w_ukv.astype(BF16),
        g_qn=head_vec(g_qn), g_kn=head_vec(g_kn), w_pool=w_pool.astype(BF16),
        pool_scale=pool_scale.reshape(DEPTH, 1, POOL_W), conv_w=conv_w, cos=cos, sin=sin)


def kernel(x_prompt, x_sample, cache_ckv, cache_krope, c, c_ctx, w_ada, b_ada, g_mix, g_ffn, w_in, g_q_a, w_uq,
           g_kv_a, w_ukv, g_qn, g_kn, w_pool, pool_scale, conv_w, w_out, w_gate, w_up, w_down):
    W = _prep_weights(g_mix, g_ffn, w_in, g_q_a, w_uq, g_kv_a, w_ukv, g_qn, g_kn, w_pool, pool_scale, conv_w)
    m_all = jnp.concatenate([c_ctx[None, :], c, jnp.zeros((MOD_ROWS - 1 - DEC_BATCH, D_MODEL), F32)], axis=0)
    mod4 = _ada(m_all, w_ada, b_ada).reshape(DEPTH, MOD_ROWS, 1, N_MOD)
    cache_kr_pad = jnp.pad(cache_krope, ((0, 0), (0, 0), (0, 0), (0, LANE - ROPE_DIM)))
    kc, vc = _cache_kv(cache_ckv, cache_kr_pad, W)

    xs = (x_prompt.reshape(N_CTX, D_MODEL), x_sample.reshape(N_DEC, D_MODEL))
    new_ckv, new_kr = [], []
    for l in range(DEPTH):
        q, k, v, ckvn, kr, pin, cv = _in_proj(l, xs, mod4, W)
        att_ctx = _attn_ctx(l, q, k, v)
        att_dec, *ffn_w, w_out_b = _attn_dec(l, q, k, v, kc, vc, w_gate, w_up, w_down, w_out)
        pc = _mixers(l, pin, cv, W)
        x_all, h_all = _out_proj(l, xs, att_ctx, att_dec, pc, w_out_b, mod4, W)
        if l < DEPTH - 1:
            xs = (_ffn(l, x_all, h_all, mod4, ffn_w),)
        else:
            y_ctx = _ffn(l, x_all, h_all, mod4, ffn_w, 0, N_CTX)
            y_dec = _ffn(l, x_all, h_all, mod4, ffn_w, N_CTX, N_DEC)
        new_ckv.append(ckvn[:N_CTX].reshape(BATCH, SEQ, KV_LORA))
        new_kr.append(kr[:N_CTX].reshape(BATCH, SEQ, ROPE_DIM))
    y_prompt = y_ctx.reshape(BATCH, SEQ, D_MODEL)
    y_sample = y_dec.reshape(DEC_BATCH, DEC_SEQ, D_MODEL)
    return y_prompt, y_sample, jnp.stack(new_ckv, axis=1), jnp.stack(new_kr, axis=1)
```

```python
import functools

import jax
import jax.numpy as jnp
import numpy as np
from jax import lax
from jax.experimental import pallas as pl
from jax.experimental.pallas import tpu as pltpu

D_MODEL = 2048
BATCH = 16
SEQ = 256
DEPTH = 4
DEC_BATCH = 4
DEC_SEQ = 2048
PAST_LEN = 512
GRID_W = 64
N_HEADS = 8
QK_NOPE = 128
ROPE_DIM = 64
QK_DIM = QK_NOPE + ROPE_DIM
V_DIM = 128
Q_LORA = 512
KV_LORA = 256
POOL_WINDOWS = (2, 4, 8, 16)
POOL_GC = 128
POOL_W = 512
CONV_W = 512
D_FF = 5632
ROPE_BASE = 10000.0
EPS = 1e-6

N_CTX = BATCH * SEQ
N_DEC = DEC_BATCH * DEC_SEQ
N_TOK = N_CTX + N_DEC
N_MOD = 6 * D_MODEL
MOD_ROWS = 8

LANE = 128
HEAD_PAD = 2 * LANE
IN_W = Q_LORA + KV_LORA + ROPE_DIM + POOL_W + 3 * CONV_W
IN_CQ = 0
IN_CKV = IN_CQ + Q_LORA
IN_KR = IN_CKV + KV_LORA
IN_A = IN_KR + LANE
IN_A_SRC = IN_KR + ROPE_DIM
IN_B = POOL_W + 3 * CONV_W

TM_IN = 512
SUB_IN = 256
TM_OUT = 512
SUB_OUT = 256
TM_FFN = 1024
TF_FFN = 512
SUB_FFN = 256
TN_ADA = 1024
NB_CTX = 2
TS_MIX = 2048
HALO = 8
TQ_DEC = 256
TR_WIN = 256
VMEM_LIMIT = 56 * 1024 * 1024
VMEM_LIMIT_BIG = 62 * 1024 * 1024

F32 = jnp.float32
BF16 = jnp.bfloat16
Q_SCALE = QK_DIM ** -0.5 * 1.4426950408889634


def _mod_row(i, tm):
    n_ctx_tiles = N_CTX // tm
    return jnp.where(i < n_ctx_tiles, 0, 1 + (i - n_ctx_tiles) // (DEC_SEQ // tm))


def _params(sem, vmem=VMEM_LIMIT):
    return pltpu.CompilerParams(dimension_semantics=sem, vmem_limit_bytes=vmem)


def _silu(x):
    return x / (1.0 + jnp.exp(-x))


def _rms(x, n):
    return lax.rsqrt(jnp.sum(x * x, axis=-1, keepdims=True) * (1.0 / n) + EPS)


def _token_specs(tm, width, split):
    if not split:
        return [pl.BlockSpec((tm, width), lambda i, *_: (i, 0))]
    n_ctx_tiles = N_CTX // tm
    return [pl.BlockSpec((tm, width), lambda i, *_: (jnp.minimum(i, n_ctx_tiles - 1), 0)),
            pl.BlockSpec((tm, width), lambda i, *_: (jnp.maximum(i - n_ctx_tiles, 0), 0))]


def _token_rows(x_refs, rows, tm):
    if len(x_refs) == 1:
        return x_refs[0][rows, :]
    return jnp.where(pl.program_id(0) < N_CTX // tm, x_refs[0][rows, :], x_refs[1][rows, :])


def _ada_kernel(m_ref, w_ref, b_ref, o_ref):
    a = _silu(m_ref[...]).astype(BF16)
    o_ref[...] = jnp.dot(a, w_ref[...].astype(BF16), preferred_element_type=F32) + b_ref[...]


def _ada(m_all, w_ada, b_ada):
    return pl.pallas_call(
        _ada_kernel,
        grid=(DEPTH, N_MOD // TN_ADA),
        in_specs=[
            pl.BlockSpec((MOD_ROWS, D_MODEL), lambda l, j: (0, 0)),
            pl.BlockSpec((None, D_MODEL, TN_ADA), lambda l, j: (l, 0, j)),
            pl.BlockSpec((None, 1, TN_ADA), lambda l, j: (l, 0, j)),
        ],
        out_specs=pl.BlockSpec((None, MOD_ROWS, TN_ADA), lambda l, j: (l, 0, j)),
        out_shape=jax.ShapeDtypeStruct((DEPTH, MOD_ROWS, N_MOD), F32),
        compiler_params=_params(("parallel", "parallel")),
        name="ada_mod",
    )(m_all, w_ada, b_ada.reshape(DEPTH, 1, N_MOD))


def _w_in_kernel(wt_ref, a_ref, b_ref):
    a = wt_ref[:IN_A, :].T
    lane = lax.broadcasted_iota(jnp.int32, a.shape, 1)
    a_ref[...] = jnp.where(lane < IN_A_SRC, a, 0.0).astype(BF16)
    b_ref[...] = wt_ref[IN_A_SRC:, :].T.astype(BF16)


def _prep_w_in(w_in):
    return pl.pallas_call(
        _w_in_kernel,
        grid=(DEPTH, D_MODEL // TR_WIN),
        in_specs=[pl.BlockSpec((None, IN_W, TR_WIN), lambda l, r: (l, 0, r))],
        out_specs=[pl.BlockSpec((None, TR_WIN, IN_A), lambda l, r: (l, r, 0)),
                   pl.BlockSpec((None, TR_WIN, IN_B), lambda l, r: (l, r, 0))],
        out_shape=[jax.ShapeDtypeStruct((DEPTH, D_MODEL, IN_A), BF16),
                   jax.ShapeDtypeStruct((DEPTH, D_MODEL, IN_B), BF16)],
        compiler_params=_params(("parallel", "parallel")),
        name="prep_w_in",
    )(jnp.swapaxes(w_in, 1, 2))


def _rope_tables():
    f32 = np.float32
    t = np.arange(DEC_SEQ)
    r_pos = (t // GRID_W).astype(f32)
    c_pos = (t % GRID_W).astype(f32)
    nf = ROPE_DIM // 4
    inv = f32(ROPE_BASE) ** (-np.arange(nf, dtype=f32) / f32(nf))
    ang_r = r_pos[:, None] * inv[None, :]
    ang_c = c_pos[:, None] * inv[None, :]
    zeros = np.zeros((DEC_SEQ, LANE - ROPE_DIM), f32)
    cos = np.concatenate([np.cos(ang_r), np.cos(ang_r), np.cos(ang_c), np.cos(ang_c), zeros], axis=1)
    sin = np.concatenate([-np.sin(ang_r), np.sin(ang_r), -np.sin(ang_c), np.sin(ang_c), zeros], axis=1)
    ident_cos = np.concatenate([np.ones((TM_IN, ROPE_DIM), f32), np.zeros((TM_IN, LANE - ROPE_DIM), f32)], axis=1)
    ident_sin = np.zeros((TM_IN, LANE), f32)
    return (jnp.asarray(np.concatenate([ident_cos, cos], axis=0).astype(f32)),
            jnp.asarray(np.concatenate([ident_sin, sin], axis=0).astype(f32)))


def _rope(x, cos, sin, first_half):
    n = x.shape[-1]
    swapped = jnp.where(first_half, pltpu.roll(x, n - ROPE_DIM // 4, 1), pltpu.roll(x, ROPE_DIM // 4, 1))
    return x * cos + swapped * sin


def _in_kernel(*refs, n_x):
    x_refs, refs = refs[:n_x], refs[n_x:]
    (sh_ref, sc_ref, gmix_ref, wa_ref, wb_ref, gqa_ref, wuq_ref, gqn_ref, gkva_ref, wukv_ref, gkn_ref,
     cos_ref, sin_ref, q_ref, k_ref, v_ref, ckv_ref, kr_ref, pin_ref, cv_ref, hb_scr) = refs
    lane = lax.broadcasted_iota(jnp.int32, (SUB_IN, LANE), 1)
    first_half = (lane & (ROPE_DIM // 4)) == 0
    gqn = gqn_ref[...]
    gkn = gkn_ref[...]

    for r0 in range(0, TM_IN, SUB_IN):
        rows = slice(r0, r0 + SUB_IN)
        x = _token_rows(x_refs, rows, TM_IN)
        h = x * _rms(x, D_MODEL) * gmix_ref[...]
        h = h * (1.0 + sc_ref[...]) + sh_ref[...]
        hb = h.astype(BF16)
        hb_scr[rows, :] = hb
        u = jnp.dot(hb, wa_ref[...], preferred_element_type=F32)
        cos = cos_ref[rows, :]
        sin = sin_ref[rows, :]

        cq = u[:, IN_CQ:IN_CQ + Q_LORA]
        cqn = cq * _rms(cq, Q_LORA) * gqa_ref[...]
        q = jnp.dot(cqn.astype(BF16), wuq_ref[...], preferred_element_type=F32)
        for hd in range(N_HEADS):
            qh = q[:, hd * HEAD_PAD:(hd + 1) * HEAD_PAD]
            qh = qh * (_rms(qh, QK_DIM) * Q_SCALE) * gqn
            q_ref[hd, rows, :LANE] = qh[:, :LANE].astype(BF16)
            q_ref[hd, rows, LANE:] = (
                _rope(qh[:, LANE:], cos, sin, first_half).astype(BF16))

        ckv = u[:, IN_CKV:IN_CKV + KV_LORA]
        ckvn = ckv * _rms(ckv, KV_LORA) * gkva_ref[...]
        ckv_ref[rows, :] = ckvn
        kv = jnp.dot(ckvn.astype(BF16), wukv_ref[...], preferred_element_type=F32)
        krb = u[:, IN_KR:IN_KR + LANE]
        kr_ref[rows, :] = krb[:, :ROPE_DIM]
        kr_ss = jnp.sum(krb * krb, axis=-1, keepdims=True)
        krg = _rope(krb * gkn[:, LANE:], cos, sin, first_half)
        for hd in range(N_HEADS):
            kn = kv[:, hd * HEAD_PAD:hd * HEAD_PAD + LANE]
            rs = lax.rsqrt((jnp.sum(kn * kn, axis=-1, keepdims=True) + kr_ss) * (1.0 / QK_DIM) + EPS)
            k_ref[hd, rows, :LANE] = (kn * rs * gkn[:, :LANE]).astype(BF16)
            k_ref[hd, rows, LANE:] = (krg * rs).astype(BF16)
            v_ref[hd, rows, :] = kv[:, hd * HEAD_PAD + LANE:(hd + 1) * HEAD_PAD].astype(BF16)

    um = jnp.dot(hb_scr[...], wb_ref[...], preferred_element_type=F32)
    pin_ref[...] = um[:, :POOL_W].astype(BF16)
    cv_ref[...] = um[:, POOL_W:].astype(BF16)


def _in_proj(l, xs, mod4, W):
    tm = TM_IN
    n_ctx_tiles = N_CTX // tm
    row = functools.partial(_mod_row, tm=tm)
    vec = lambda w: pl.BlockSpec((None, 1, w), lambda i: (l, 0, 0))
    mat = lambda r, c: pl.BlockSpec((None, r, c), lambda i: (l, 0, 0), pipeline_mode=pl.Buffered(1))
    tab = pl.BlockSpec((tm, LANE), lambda i: (jnp.where(i < n_ctx_tiles, 0, 1 + (i - n_ctx_tiles) % (DEC_SEQ // tm)), 0))
    tok = lambda w: pl.BlockSpec((tm, w), lambda i: (i, 0))
    head = lambda w: pl.BlockSpec((N_HEADS, tm, w), lambda i: (0, i, 0))
    return pl.pallas_call(
        functools.partial(_in_kernel, n_x=len(xs)),
        grid=(N_TOK // tm,),
        in_specs=_token_specs(tm, D_MODEL, len(xs) == 2) + [
            pl.BlockSpec((None, None, 1, D_MODEL), lambda i: (l, row(i), 0, 0)),
            pl.BlockSpec((None, None, 1, D_MODEL), lambda i: (l, row(i), 0, 1)),
            vec(D_MODEL), mat(D_MODEL, IN_A), mat(D_MODEL, IN_B), vec(Q_LORA), mat(Q_LORA, N_HEADS * HEAD_PAD),
            vec(HEAD_PAD),
            vec(KV_LORA), mat(KV_LORA, N_HEADS * HEAD_PAD), vec(HEAD_PAD), tab, tab,
        ],
        out_specs=[head(HEAD_PAD), head(HEAD_PAD), head(V_DIM), tok(KV_LORA),
                   tok(ROPE_DIM), tok(POOL_W), tok(3 * CONV_W)],
        out_shape=[
            jax.ShapeDtypeStruct((N_HEADS, N_TOK, HEAD_PAD), BF16),
            jax.ShapeDtypeStruct((N_HEADS, N_TOK, HEAD_PAD), BF16),
            jax.ShapeDtypeStruct((N_HEADS, N_TOK, V_DIM), BF16),
            jax.ShapeDtypeStruct((N_TOK, KV_LORA), F32),
            jax.ShapeDtypeStruct((N_TOK, ROPE_DIM), F32),
            jax.ShapeDtypeStruct((N_TOK, POOL_W), BF16),
            jax.ShapeDtypeStruct((N_TOK, 3 * CONV_W), BF16),
        ],
        scratch_shapes=[pltpu.VMEM((tm, D_MODEL), BF16)],
        compiler_params=_params(("parallel",), VMEM_LIMIT_BIG if len(xs) == 2 else VMEM_LIMIT),
        name=f"in_proj_{l}",
    )(*xs, mod4, mod4, W["g_mix"], W["w_in_a"], W["w_in_b"], W["g_q_a"], W["w_uq"], W["g_qn"], W["g_kv_a"],
      W["w_ukv"], W["g_kn"], W["cos"], W["sin"])


def _kvc_kernel(ckv_ref, kr_ref, wukv_ref, gkn_ref, k_ref, v_ref):
    kv = jnp.dot(ckv_ref[...].astype(BF16), wukv_ref[...], preferred_element_type=F32)
    krb = kr_ref[...]
    kr_ss = jnp.sum(krb * krb, axis=-1, keepdims=True)
    gkn = gkn_ref[...]
    krg = krb * gkn[:, LANE:]
    for hd in range(N_HEADS):
        kn = kv[:, hd * HEAD_PAD:hd * HEAD_PAD + LANE]
        rs = lax.rsqrt((jnp.sum(kn * kn, axis=-1, keepdims=True) + kr_ss) * (1.0 / QK_DIM) + EPS)
        k_ref[hd, :, :LANE] = (kn * rs * gkn[:, :LANE]).astype(BF16)
        k_ref[hd, :, LANE:] = (krg * rs).astype(BF16)
        v_ref[hd, :, :] = kv[:, hd * HEAD_PAD + LANE:(hd + 1) * HEAD_PAD].astype(BF16)


def _cache_kv(cache_ckv, cache_kr_pad, W):
    return pl.pallas_call(
        _kvc_kernel,
        grid=(DEPTH, DEC_BATCH),
        in_specs=[
            pl.BlockSpec((None, None, PAST_LEN, KV_LORA), lambda l, b: (b, l, 0, 0)),
            pl.BlockSpec((None, None, PAST_LEN, LANE), lambda l, b: (b, l, 0, 0)),
            pl.BlockSpec((None, KV_LORA, N_HEADS * HEAD_PAD), lambda l, b: (l, 0, 0)),
            pl.BlockSpec((None, 1, HEAD_PAD), lambda l, b: (l, 0, 0)),
        ],
        out_specs=[
            pl.BlockSpec((None, None, N_HEADS, PAST_LEN, HEAD_PAD), lambda l, b: (l, b, 0, 0, 0)),
            pl.BlockSpec((None, None, N_HEADS, PAST_LEN, V_DIM), lambda l, b: (l, b, 0, 0, 0)),
        ],
        out_shape=[
            jax.ShapeDtypeStruct((DEPTH, DEC_BATCH, N_HEADS, PAST_LEN, HEAD_PAD), BF16),
            jax.ShapeDtypeStruct((DEPTH, DEC_BATCH, N_HEADS, PAST_LEN, V_DIM), BF16),
        ],
        compiler_params=_params(("parallel", "parallel")),
        name="cache_kv",
    )(cache_ckv, cache_kr_pad, W["w_ukv"], W["g_kn"])


def _softmax_pv(q, ks, vs):
    ss = [lax.dot_general(q, k, (((1,), (1,)), ((), ())), preferred_element_type=F32) for k in ks]
    m = functools.reduce(jnp.maximum, [jnp.max(s, axis=-1, keepdims=True) for s in ss])
    ps = [jnp.exp2(s - m) for s in ss]
    den = functools.reduce(jnp.add, [jnp.sum(p, axis=-1, keepdims=True) for p in ps])
    o = functools.reduce(jnp.add, [jnp.dot(p.astype(BF16), v, preferred_element_type=F32) for p, v in zip(ps, vs)])
    return o / den


def _attn_ctx_kernel(q_ref, k_ref, v_ref, o_ref):
    for r0 in range(0, NB_CTX * SEQ, SEQ):
        rows = slice(r0, r0 + SEQ)
        for hd in range(N_HEADS):
            q = q_ref[hd, rows, :]
            k = k_ref[hd, rows, :]
            v = v_ref[hd, rows, :]
            o_ref[rows, hd * V_DIM:(hd + 1) * V_DIM] = _softmax_pv(q, [k], [v]).astype(BF16)


def _attn_ctx(l, q, k, v):
    rows = NB_CTX * SEQ
    return pl.pallas_call(
        _attn_ctx_kernel,
        grid=(BATCH // NB_CTX,),
        in_specs=[
            pl.BlockSpec((N_HEADS, rows, HEAD_PAD), lambda b: (0, b, 0)),
            pl.BlockSpec((N_HEADS, rows, HEAD_PAD), lambda b: (0, b, 0)),
            pl.BlockSpec((N_HEADS, rows, V_DIM), lambda b: (0, b, 0)),
        ],
        out_specs=pl.BlockSpec((rows, N_HEADS * V_DIM), lambda b: (b, 0)),
        out_shape=jax.ShapeDtypeStruct((N_CTX, N_HEADS * V_DIM), BF16),
        compiler_params=_params(("parallel",)),
        name=f"attn_ctx_{l}",
    )(q, k, v)


def _attn_dec_kernel(q_ref, k_ref, v_ref, kc_ref, vc_ref, wg_ref, wu_ref, wd_ref, wo_ref,
                     o_ref, wgb_ref, wub_ref, wdb_ref, wob_ref):
    ks = [k_ref[...], kc_ref[...]]
    vs = [v_ref[...], vc_ref[...]]
    for r in range(0, DEC_SEQ, TQ_DEC):
        o_ref[r:r + TQ_DEC, :] = _softmax_pv(q_ref[r:r + TQ_DEC, :], ks, vs).astype(BF16)
    wgb_ref[...] = wg_ref[...].astype(BF16)
    wub_ref[...] = wu_ref[...].astype(BF16)
    wdb_ref[...] = wd_ref[...].astype(BF16)
    wob_ref[...] = wo_ref[...].astype(BF16)


def _attn_dec(l, q, k, v, kc, vc, w_gate, w_up, w_down, w_out):
    first = N_CTX // DEC_SEQ
    steps = DEC_BATCH * N_HEADS
    slab = lambda rows, cols: pl.BlockSpec((None, rows // steps, cols), lambda b, h: (l, b * N_HEADS + h, 0))
    slab_out = lambda rows, cols: pl.BlockSpec((rows // steps, cols), lambda b, h: (b * N_HEADS + h, 0))
    return pl.pallas_call(
        _attn_dec_kernel,
        grid=(DEC_BATCH, N_HEADS),
        in_specs=[
            pl.BlockSpec((None, DEC_SEQ, HEAD_PAD), lambda b, h: (h, first + b, 0)),
            pl.BlockSpec((None, DEC_SEQ, HEAD_PAD), lambda b, h: (h, first + b, 0)),
            pl.BlockSpec((None, DEC_SEQ, V_DIM), lambda b, h: (h, first + b, 0)),
            pl.BlockSpec((None, None, None, PAST_LEN, HEAD_PAD), lambda b, h: (l, b, h, 0, 0)),
            pl.BlockSpec((None, None, None, PAST_LEN, V_DIM), lambda b, h: (l, b, h, 0, 0)),
            slab(D_MODEL, D_FF), slab(D_MODEL, D_FF), slab(D_FF, D_MODEL), slab(D_MODEL, D_MODEL),
        ],
        out_specs=[pl.BlockSpec((DEC_SEQ, V_DIM), lambda b, h: (b, h)),
                   slab_out(D_MODEL, D_FF), slab_out(D_MODEL, D_FF), slab_out(D_FF, D_MODEL),
                   slab_out(D_MODEL, D_MODEL)],
        out_shape=[jax.ShapeDtypeStruct((N_DEC, N_HEADS * V_DIM), BF16),
                   jax.ShapeDtypeStruct((D_MODEL, D_FF), BF16), jax.ShapeDtypeStruct((D_MODEL, D_FF), BF16),
                   jax.ShapeDtypeStruct((D_FF, D_MODEL), BF16), jax.ShapeDtypeStruct((D_MODEL, D_MODEL), BF16)],
        compiler_params=_params(("parallel", "parallel")),
        name=f"attn_dec_{l}",
    )(q, k, v, kc, vc, w_gate, w_up, w_down, w_out)


def _mix_block(pin_ref, cv_ref, wp_ref, ps_ref, cw_ref, o_ref, pad_ref, seq):
    n_seq = TS_MIX // seq
    zeros = jnp.zeros((n_seq, HALO, LANE), F32)
    pad_ref[:, :HALO, :] = zeros
    pad_ref[:, HALO + seq:, :] = zeros
    pos = lax.broadcasted_iota(jnp.int32, (seq, LANE), 0)

    def put(x):
        pad_ref[:, HALO:HALO + seq, :] = x.reshape(n_seq, seq, LANE)

    def at(j):
        return pad_ref[:, HALO + j:HALO + j + seq, :]

    for g, w in enumerate(POOL_WINDOWS):
        half = w // 2
        u = pin_ref[:, g * POOL_GC:(g + 1) * POOL_GC].astype(F32)
        put(u)
        total = functools.reduce(jnp.add, [at(j) for j in range(-half, half)])
        cnt = (jnp.minimum(pos + half, seq) - jnp.maximum(pos - half, 0)).astype(F32)
        d = (total / cnt).reshape(TS_MIX, LANE) - u
        y = jnp.dot(d.astype(BF16), wp_ref[g], preferred_element_type=F32)
        o_ref[:, g * POOL_GC:(g + 1) * POOL_GC] = (y * ps_ref[:, g * POOL_GC:(g + 1) * POOL_GC]).astype(BF16)

    for g in range(CONV_W // LANE):
        sl = slice(g * LANE, (g + 1) * LANE)
        ch = cv_ref[:, g * LANE:(g + 1) * LANE].astype(F32)
        cb = cv_ref[:, CONV_W + g * LANE:CONV_W + (g + 1) * LANE].astype(F32)
        cc = cv_ref[:, 2 * CONV_W + g * LANE:2 * CONV_W + (g + 1) * LANE].astype(F32)
        put(cc * ch)
        conv = at(-1) * cw_ref[0:1, sl] + at(0) * cw_ref[1:2, sl] + at(1) * cw_ref[2:3, sl]
        o_ref[:, POOL_W + g * LANE:POOL_W + (g + 1) * LANE] = (cb * conv.reshape(TS_MIX, LANE)).astype(BF16)


def _mix_kernel(pin_ref, cv_ref, wp_ref, ps_ref, cw_ref, o_ref, pad_ctx_ref, pad_dec_ref):
    i = pl.program_id(0)

    @pl.when(i < N_CTX // TS_MIX)
    def _():
        _mix_block(pin_ref, cv_ref, wp_ref, ps_ref, cw_ref, o_ref, pad_ctx_ref, SEQ)

    @pl.when(i >= N_CTX // TS_MIX)
    def _():
        _mix_block(pin_ref, cv_ref, wp_ref, ps_ref, cw_ref, o_ref, pad_dec_ref, DEC_SEQ)


def _mixers(l, pin, cv, W):
    return pl.pallas_call(
        _mix_kernel,
        grid=(N_TOK // TS_MIX,),
        in_specs=[
            pl.BlockSpec((TS_MIX, POOL_W), lambda i: (i, 0)),
            pl.BlockSpec((TS_MIX, 3 * CONV_W), lambda i: (i, 0)),
            pl.BlockSpec((None, len(POOL_WINDOWS), POOL_GC, POOL_GC), lambda i: (l, 0, 0, 0)),
            pl.BlockSpec((None, 1, POOL_W), lambda i: (l, 0, 0)),
            pl.BlockSpec((None, 3, CONV_W), lambda i: (l, 0, 0)),
        ],
        out_specs=pl.BlockSpec((TS_MIX, POOL_W + CONV_W), lambda i: (i, 0)),
        out_shape=jax.ShapeDtypeStruct((N_TOK, POOL_W + CONV_W), BF16),
        scratch_shapes=[pltpu.VMEM((TS_MIX // SEQ, SEQ + 2 * HALO, LANE), F32),
                        pltpu.VMEM((TS_MIX // DEC_SEQ, DEC_SEQ + 2 * HALO, LANE), F32)],
        compiler_params=_params(("parallel",), VMEM_LIMIT_BIG),
        name=f"mixers_{l}",
    )(pin, cv, W["w_pool"], W["pool_scale"], W["conv_w"])


def _out_kernel(*refs, n_x):
    x_refs, refs = refs[:n_x], refs[n_x:]
    actx_ref, adec_ref, pc_ref, wo_ref, g1_ref, sh_ref, sc_ref, gffn_ref, o_ref, h_ref = refs
    i = pl.program_id(0)
    att_w = N_HEADS * V_DIM

    def finish(att_ref):
        for r0 in range(0, TM_OUT, SUB_OUT):
            rows = slice(r0, r0 + SUB_OUT)
            y = jnp.dot(att_ref[rows, :], wo_ref[:att_w, :], preferred_element_type=F32)
            y = y + jnp.dot(pc_ref[rows, :], wo_ref[att_w:, :], preferred_element_type=F32)
            x = _token_rows(x_refs, rows, TM_OUT) + g1_ref[...] * y
            o_ref[rows, :] = x
            h = x * _rms(x, D_MODEL) * gffn_ref[...]
            h_ref[rows, :] = (h * (1.0 + sc_ref[...]) + sh_ref[...]).astype(BF16)

    @pl.when(i < N_CTX // TM_OUT)
    def _():
        finish(actx_ref)

    @pl.when(i >= N_CTX // TM_OUT)
    def _():
        finish(adec_ref)


def _out_proj(l, xs, att_ctx, att_dec, pc, w_out_b, mod4, W):
    tm = TM_OUT
    n_ctx_tiles = N_CTX // tm
    row = functools.partial(_mod_row, tm=tm)
    att_w = N_HEADS * V_DIM
    return pl.pallas_call(
        functools.partial(_out_kernel, n_x=len(xs)),
        grid=(N_TOK // tm,),
        in_specs=_token_specs(tm, D_MODEL, len(xs) == 2) + [
            pl.BlockSpec((tm, att_w), lambda i: (jnp.minimum(i, n_ctx_tiles - 1), 0)),
            pl.BlockSpec((tm, att_w), lambda i: (jnp.maximum(i - n_ctx_tiles, 0), 0)),
            pl.BlockSpec((tm, POOL_W + CONV_W), lambda i: (i, 0)),
            pl.BlockSpec((D_MODEL, D_MODEL), lambda i: (0, 0), pipeline_mode=pl.Buffered(1)),
            pl.BlockSpec((None, None, 1, D_MODEL), lambda i: (l, row(i), 0, 2)),
            pl.BlockSpec((None, None, 1, D_MODEL), lambda i: (l, row(i), 0, 3)),
            pl.BlockSpec((None, None, 1, D_MODEL), lambda i: (l, row(i), 0, 4)),
            pl.BlockSpec((None, 1, D_MODEL), lambda i: (l, 0, 0)),
        ],
        out_specs=[pl.BlockSpec((tm, D_MODEL), lambda i: (i, 0)), pl.BlockSpec((tm, D_MODEL), lambda i: (i, 0))],
        out_shape=[jax.ShapeDtypeStruct((N_TOK, D_MODEL), F32), jax.ShapeDtypeStruct((N_TOK, D_MODEL), BF16)],
        compiler_params=_params(("parallel",)),
        name=f"out_proj_{l}",
    )(*xs, att_ctx, att_dec, pc, w_out_b, mod4, mod4, mod4, W["g_ffn"])


def _ffn_kernel(x_ref, h_ref, g2_ref, wg_ref, wu_ref, wd_ref, o_ref):
    f = pl.program_id(1)

    @pl.when(f == 0)
    def _():
        o_ref[...] = x_ref[...]

    h = h_ref[...]
    y = None
    for c0 in range(0, TF_FFN, SUB_FFN):
        gate = jnp.dot(h, wg_ref[:, c0:c0 + SUB_FFN], preferred_element_type=F32)
        up = jnp.dot(h, wu_ref[:, c0:c0 + SUB_FFN], preferred_element_type=F32)
        part = jnp.dot((_silu(gate) * up).astype(BF16), wd_ref[c0:c0 + SUB_FFN, :], preferred_element_type=F32)
        y = part if y is None else y + part
    o_ref[...] += g2_ref[...] * y


def _ffn(l, x_all, h_all, mod4, ffn_w, row0=0, n_rows=N_TOK):
    tm, tf = TM_FFN, TF_FFN
    t0 = row0 // tm
    row = lambda i: _mod_row(i + t0, tm)
    return pl.pallas_call(
        _ffn_kernel,
        grid=(n_rows // tm, D_FF // tf),
        in_specs=[
            pl.BlockSpec((tm, D_MODEL), lambda i, f: (i + t0, 0)),
            pl.BlockSpec((tm, D_MODEL), lambda i, f: (i + t0, 0)),
            pl.BlockSpec((None, None, 1, D_MODEL), lambda i, f: (l, row(i), 0, 5)),
            pl.BlockSpec((D_MODEL, tf), lambda i, f: (0, f)),
            pl.BlockSpec((D_MODEL, tf), lambda i, f: (0, f)),
            pl.BlockSpec((tf, D_MODEL), lambda i, f: (f, 0)),
        ],
        out_specs=pl.BlockSpec((tm, D_MODEL), lambda i, f: (i, 0)),
        out_shape=jax.ShapeDtypeStruct((n_rows, D_MODEL), F32),
        compiler_params=_params(("parallel", "arbitrary"), VMEM_LIMIT_BIG),
        name=f"ffn_{l}_{row0}",
    )(x_all, h_all, mod4, *ffn_w)


def _prep_weights(g_mix, g_ffn, w_in, g_q_a, w_uq, g_kv_a, w_ukv, g_qn, g_kn, w_pool, pool_scale, conv_w):
    w_in_a, w_in_b = _prep_w_in(w_in)
    w_uq_p = jnp.pad(w_uq.reshape(DEPTH, Q_LORA, N_HEADS, QK_DIM), ((0, 0), (0, 0), (0, 0), (0, HEAD_PAD - QK_DIM)))
    head_vec = lambda g: jnp.pad(g, ((0, 0), (0, HEAD_PAD - QK_DIM))).reshape(DEPTH, 1, HEAD_PAD)
    cos, sin = _rope_tables()
    return dict(
        g_mix=g_mix.reshape(DEPTH, 1, D_MODEL), g_ffn=g_ffn.reshape(DEPTH, 1, D_MODEL),
        w_in_a=w_in_a, w_in_b=w_in_b, g_q_a=g_q_a.reshape(DEPTH, 1, Q_LORA),
        w_uq=w_uq_p.reshape(DEPTH, Q_LORA, N_HEADS * HEAD_PAD).astype(BF16),
        g_kv_a=g_kv_a.reshape(DEPTH, 1, KV_LORA), w_ukv=w_ukv.astype(BF16),
        g_qn=head_vec(g_qn), g_kn=head_vec(g_kn), w_pool=w_pool.astype(BF16),
        pool_scale=pool_scale.reshape(DEPTH, 1, POOL_W), conv_w=conv_w, cos=cos, sin=sin)


def kernel(x_prompt, x_sample, cache_ckv, cache_krope, c, c_ctx, w_ada, b_ada, g_mix, g_ffn, w_in, g_q_a, w_uq,
           g_kv_a, w_ukv, g_qn, g_kn, w_pool, pool_scale, conv_w, w_out, w_gate, w_up, w_down):
    W = _prep_weights(g_mix, g_ffn, w_in, g_q_a, w_uq, g_kv_a, w_ukv, g_qn, g_kn, w_pool, pool_scale, conv_w)
    m_all = jnp.concatenate([c_ctx[None, :], c, jnp.zeros((MOD_ROWS - 1 - DEC_BATCH, D_MODEL), F32)], axis=0)
    mod4 = _ada(m_all, w_ada, b_ada).reshape(DEPTH, MOD_ROWS, 1, N_MOD)
    cache_kr_pad = jnp.pad(cache_krope, ((0, 0), (0, 0), (0, 0), (0, LANE - ROPE_DIM)))
    kc, vc = _cache_kv(cache_ckv, cache_kr_pad, W)

    xs = (x_prompt.reshape(N_CTX, D_MODEL), x_sample.reshape(N_DEC, D_MODEL))
    new_ckv, new_kr = [], []
    for l in range(DEPTH):
        q, k, v, ckvn, kr, pin, cv = _in_proj(l, xs, mod4, W)
        att_ctx = _attn_ctx(l, q, k, v)
        att_dec, *ffn_w, w_out_b = _attn_dec(l, q, k, v, kc, vc, w_gate, w_up, w_down, w_out)
        pc = _mixers(l, pin, cv, W)
        x_all, h_all = _out_proj(l, xs, att_ctx, att_dec, pc, w_out_b, mod4, W)
        if l < DEPTH - 1:
            xs = (_ffn(l, x_all, h_all, mod4, ffn_w),)
        else:
            y_ctx = _ffn(l, x_all, h_all, mod4, ffn_w, 0, N_CTX)
            y_dec = _ffn(l, x_all, h_all, mod4, ffn_w, N_CTX, N_DEC)
        new_ckv.append(ckvn[:N_CTX].reshape(BATCH, SEQ, KV_LORA))
        new_kr.append(kr[:N_CTX].reshape(BATCH, SEQ, ROPE_DIM))
    y_prompt = y_ctx.reshape(BATCH, SEQ, D_MODEL)
    y_sample = y_dec.reshape(DEC_BATCH, DEC_SEQ, D_MODEL)
    return y_prompt, y_sample, jnp.stack(new_ckv, axis=1), jnp.stack(new_kr, axis=1)
```

```python
import functools

import jax
import jax.numpy as jnp
import numpy as np
from jax import lax
from jax.experimental import pallas as pl
from jax.experimental.pallas import tpu as pltpu

D_MODEL = 2048
BATCH = 16
SEQ = 256
DEPTH = 4
DEC_BATCH = 4
DEC_SEQ = 2048
PAST_LEN = 512
GRID_W = 64
N_HEADS = 8
QK_NOPE = 128
ROPE_DIM = 64
QK_DIM = QK_NOPE + ROPE_DIM
V_DIM = 128
Q_LORA = 512
KV_LORA = 256
POOL_WINDOWS = (2, 4, 8, 16)
POOL_GC = 128
POOL_W = 512
CONV_W = 512
D_FF = 5632
ROPE_BASE = 10000.0
EPS = 1e-6

N_CTX = BATCH * SEQ
N_DEC = DEC_BATCH * DEC_SEQ
N_TOK = N_CTX + N_DEC
N_MOD = 6 * D_MODEL
MOD_ROWS = 8

LANE = 128
HEAD_PAD = 2 * LANE
IN_W = Q_LORA + KV_LORA + ROPE_DIM + POOL_W + 3 * CONV_W
IN_CQ = 0
IN_CKV = IN_CQ + Q_LORA
IN_KR = IN_CKV + KV_LORA
IN_A = IN_KR + LANE
IN_A_SRC = IN_KR + ROPE_DIM
IN_B = POOL_W + 3 * CONV_W

TM_IN = 512
SUB_IN = 256
TM_OUT = 512
SUB_OUT = 256
TM_FFN = 1024
TF_FFN = 512
SUB_FFN = 256
TN_ADA = 1024
TS_MIX = 2048
HALO = 8
NB_CTX = 4
TQ_DEC = 256
TR_WIN = 256
VMEM_LIMIT = 56 * 1024 * 1024
VMEM_LIMIT_BIG = 62 * 1024 * 1024

F32 = jnp.float32
BF16 = jnp.bfloat16
Q_SCALE = QK_DIM ** -0.5 * 1.4426950408889634


def _mod_row(i, tm):
    n_ctx_tiles = N_CTX // tm
    return jnp.where(i < n_ctx_tiles, 0, 1 + (i - n_ctx_tiles) // (DEC_SEQ // tm))


def _params(sem, vmem=VMEM_LIMIT):
    return pltpu.CompilerParams(dimension_semantics=sem, vmem_limit_bytes=vmem)


def _silu(x):
    return x / (1.0 + jnp.exp(-x))


def _rms(x, n):
    return lax.rsqrt(jnp.sum(x * x, axis=-1, keepdims=True) * (1.0 / n) + EPS)


def _token_specs(tm, width, split):
    if not split:
        return [pl.BlockSpec((tm, width), lambda i, *_: (i, 0))]
    n_ctx_tiles = N_CTX // tm
    return [pl.BlockSpec((tm, width), lambda i, *_: (jnp.minimum(i, n_ctx_tiles - 1), 0)),
            pl.BlockSpec((tm, width), lambda i, *_: (jnp.maximum(i - n_ctx_tiles, 0), 0))]


def _token_rows(x_refs, rows, tm):
    if len(x_refs) == 1:
        return x_refs[0][rows, :]
    return jnp.where(pl.program_id(0) < N_CTX // tm, x_refs[0][rows, :], x_refs[1][rows, :])


def _ada_kernel(m_ref, w_ref, b_ref, o_ref):
    a = _silu(m_ref[...]).astype(BF16)
    o_ref[...] = jnp.dot(a, w_ref[...].astype(BF16), preferred_element_type=F32) + b_ref[...]


def _ada(m_all, w_ada, b_ada):
    return pl.pallas_call(
        _ada_kernel,
        grid=(DEPTH, N_MOD // TN_ADA),
        in_specs=[
            pl.BlockSpec((MOD_ROWS, D_MODEL), lambda l, j: (0, 0)),
            pl.BlockSpec((None, D_MODEL, TN_ADA), lambda l, j: (l, 0, j)),
            pl.BlockSpec((None, 1, TN_ADA), lambda l, j: (l, 0, j)),
        ],
        out_specs=pl.BlockSpec((None, MOD_ROWS, TN_ADA), lambda l, j: (l, 0, j)),
        out_shape=jax.ShapeDtypeStruct((DEPTH, MOD_ROWS, N_MOD), F32),
        compiler_params=_params(("parallel", "parallel")),
        name="ada_mod",
    )(m_all, w_ada, b_ada.reshape(DEPTH, 1, N_MOD))


def _w_in_kernel(wt_ref, a_ref, b_ref):
    a = wt_ref[:IN_A, :].T
    lane = lax.broadcasted_iota(jnp.int32, a.shape, 1)
    a_ref[...] = jnp.where(lane < IN_A_SRC, a, 0.0).astype(BF16)
    b_ref[...] = wt_ref[IN_A_SRC:, :].T.astype(BF16)


def _prep_w_in(w_in):
    return pl.pallas_call(
        _w_in_kernel,
        grid=(DEPTH, D_MODEL // TR_WIN),
        in_specs=[pl.BlockSpec((None, IN_W, TR_WIN), lambda l, r: (l, 0, r))],
        out_specs=[pl.BlockSpec((None, TR_WIN, IN_A), lambda l, r: (l, r, 0)),
                   pl.BlockSpec((None, TR_WIN, IN_B), lambda l, r: (l, r, 0))],
        out_shape=[jax.ShapeDtypeStruct((DEPTH, D_MODEL, IN_A), BF16),
                   jax.ShapeDtypeStruct((DEPTH, D_MODEL, IN_B), BF16)],
        compiler_params=_params(("parallel", "parallel")),
        name="prep_w_in",
    )(jnp.swapaxes(w_in, 1, 2))


def _rope_tables():
    f32 = np.float32
    t = np.arange(DEC_SEQ)
    r_pos = (t // GRID_W).astype(f32)
    c_pos = (t % GRID_W).astype(f32)
    nf = ROPE_DIM // 4
    inv = f32(ROPE_BASE) ** (-np.arange(nf, dtype=f32) / f32(nf))
    ang_r = r_pos[:, None] * inv[None, :]
    ang_c = c_pos[:, None] * inv[None, :]
    zeros = np.zeros((DEC_SEQ, LANE - ROPE_DIM), f32)
    cos = np.concatenate([np.cos(ang_r), np.cos(ang_r), np.cos(ang_c), np.cos(ang_c), zeros], axis=1)
    sin = np.concatenate([-np.sin(ang_r), np.sin(ang_r), -np.sin(ang_c), np.sin(ang_c), zeros], axis=1)
    ident_cos = np.concatenate([np.ones((TM_IN, ROPE_DIM), f32), np.zeros((TM_IN, LANE - ROPE_DIM), f32)], axis=1)
    ident_sin = np.zeros((TM_IN, LANE), f32)
    return (jnp.asarray(np.concatenate([ident_cos, cos], axis=0).astype(f32)),
            jnp.asarray(np.concatenate([ident_sin, sin], axis=0).astype(f32)))


def _rope(x, cos, sin, first_half):
    n = x.shape[-1]
    swapped = jnp.where(first_half, pltpu.roll(x, n - ROPE_DIM // 4, 1), pltpu.roll(x, ROPE_DIM // 4, 1))
    return x * cos + swapped * sin


def _in_kernel(*refs, n_x):
    x_refs, refs = refs[:n_x], refs[n_x:]
    (sh_ref, sc_ref, gmix_ref, wa_ref, wb_ref, gqa_ref, wuq_ref, gqn_ref, gkva_ref, wukv_ref, gkn_ref,
     cos_ref, sin_ref, q_ref, k_ref, v_ref, ckv_ref, kr_ref, pin_ref, cv_ref, hb_scr) = refs
    lane = lax.broadcasted_iota(jnp.int32, (SUB_IN, LANE), 1)
    first_half = (lane & (ROPE_DIM // 4)) == 0
    gqn = gqn_ref[...]
    gkn = gkn_ref[...]

    for r0 in range(0, TM_IN, SUB_IN):
        rows = slice(r0, r0 + SUB_IN)
        x = _token_rows(x_refs, rows, TM_IN)
        h = x * _rms(x, D_MODEL) * gmix_ref[...]
        h = h * (1.0 + sc_ref[...]) + sh_ref[...]
        hb = h.astype(BF16)
        hb_scr[rows, :] = hb
        u = jnp.dot(hb, wa_ref[...], preferred_element_type=F32)
        cos = cos_ref[rows, :]
        sin = sin_ref[rows, :]

        cq = u[:, IN_CQ:IN_CQ + Q_LORA]
        cqn = cq * _rms(cq, Q_LORA) * gqa_ref[...]
        q = jnp.dot(cqn.astype(BF16), wuq_ref[...], preferred_element_type=F32)
        for hd in range(N_HEADS):
            qh = q[:, hd * HEAD_PAD:(hd + 1) * HEAD_PAD]
            qh = qh * (_rms(qh, QK_DIM) * Q_SCALE) * gqn
            q_ref[hd, rows, :LANE] = qh[:, :LANE].astype(BF16)
            q_ref[hd, rows, LANE:] = (
                _rope(qh[:, LANE:], cos, sin, first_half).astype(BF16))

        ckv = u[:, IN_CKV:IN_CKV + KV_LORA]
        ckvn = ckv * _rms(ckv, KV_LORA) * gkva_ref[...]
        ckv_ref[rows, :] = ckvn
        kv = jnp.dot(ckvn.astype(BF16), wukv_ref[...], preferred_element_type=F32)
        krb = u[:, IN_KR:IN_KR + LANE]
        kr_ref[rows, :] = krb[:, :ROPE_DIM]
        kr_ss = jnp.sum(krb * krb, axis=-1, keepdims=True)
        krg = _rope(krb * gkn[:, LANE:], cos, sin, first_half)
        for hd in range(N_HEADS):
            kn = kv[:, hd * HEAD_PAD:hd * HEAD_PAD + LANE]
            rs = lax.rsqrt((jnp.sum(kn * kn, axis=-1, keepdims=True) + kr_ss) * (1.0 / QK_DIM) + EPS)
            k_ref[hd, rows, :LANE] = (kn * rs * gkn[:, :LANE]).astype(BF16)
            k_ref[hd, rows, LANE:] = (krg * rs).astype(BF16)
            v_ref[hd, rows, :] = kv[:, hd * HEAD_PAD + LANE:(hd + 1) * HEAD_PAD].astype(BF16)

    um = jnp.dot(hb_scr[...], wb_ref[...], preferred_element_type=F32)
    pin_ref[...] = um[:, :POOL_W].astype(BF16)
    cv_ref[...] = um[:, POOL_W:].astype(BF16)


def _in_proj(l, xs, mod4, W):
    tm = TM_IN
    n_ctx_tiles = N_CTX // tm
    row = functools.partial(_mod_row, tm=tm)
    vec = lambda w: pl.BlockSpec((None, 1, w), lambda i: (l, 0, 0))
    mat = lambda r, c: pl.BlockSpec((None, r, c), lambda i: (l, 0, 0), pipeline_mode=pl.Buffered(1))
    tab = pl.BlockSpec((tm, LANE), lambda i: (jnp.where(i < n_ctx_tiles, 0, 1 + (i - n_ctx_tiles) % (DEC_SEQ // tm)), 0))
    tok = lambda w: pl.BlockSpec((tm, w), lambda i: (i, 0))
    head = lambda w: pl.BlockSpec((N_HEADS, tm, w), lambda i: (0, i, 0))
    return pl.pallas_call(
        functools.partial(_in_kernel, n_x=len(xs)),
        grid=(N_TOK // tm,),
        in_specs=_token_specs(tm, D_MODEL, len(xs) == 2) + [
            pl.BlockSpec((None, None, 1, D_MODEL), lambda i: (l, row(i), 0, 0)),
            pl.BlockSpec((None, None, 1, D_MODEL), lambda i: (l, row(i), 0, 1)),
            vec(D_MODEL), mat(D_MODEL, IN_A), mat(D_MODEL, IN_B), vec(Q_LORA), mat(Q_LORA, N_HEADS * HEAD_PAD),
            vec(HEAD_PAD),
            vec(KV_LORA), mat(KV_LORA, N_HEADS * HEAD_PAD), vec(HEAD_PAD), tab, tab,
        ],
        out_specs=[head(HEAD_PAD), head(HEAD_PAD), head(V_DIM), tok(KV_LORA),
                   tok(ROPE_DIM), tok(POOL_W), tok(3 * CONV_W)],
        out_shape=[
            jax.ShapeDtypeStruct((N_HEADS, N_TOK, HEAD_PAD), BF16),
            jax.ShapeDtypeStruct((N_HEADS, N_TOK, HEAD_PAD), BF16),
            jax.ShapeDtypeStruct((N_HEADS, N_TOK, V_DIM), BF16),
            jax.ShapeDtypeStruct((N_TOK, KV_LORA), F32),
            jax.ShapeDtypeStruct((N_TOK, ROPE_DIM), F32),
            jax.ShapeDtypeStruct((N_TOK, POOL_W), BF16),
            jax.ShapeDtypeStruct((N_TOK, 3 * CONV_W), BF16),
        ],
        scratch_shapes=[pltpu.VMEM((tm, D_MODEL), BF16)],
        compiler_params=_params(("parallel",), VMEM_LIMIT_BIG if len(xs) == 2 else VMEM_LIMIT),
        name=f"in_proj_{l}",
    )(*xs, mod4, mod4, W["g_mix"], W["w_in_a"], W["w_in_b"], W["g_q_a"], W["w_uq"], W["g_qn"], W["g_kv_a"],
      W["w_ukv"], W["g_kn"], W["cos"], W["sin"])


def _kvc_kernel(ckv_ref, kr_ref, wukv_ref, gkn_ref, k_ref, v_ref):
    gkn = gkn_ref[...]
    for b in range(DEC_BATCH):
        kv = jnp.dot(ckv_ref[b].astype(BF16), wukv_ref[...], preferred_element_type=F32)
        krb = kr_ref[b]
        kr_ss = jnp.sum(krb * krb, axis=-1, keepdims=True)
        krg = krb * gkn[:, LANE:]
        for hd in range(N_HEADS):
            kn = kv[:, hd * HEAD_PAD:hd * HEAD_PAD + LANE]
            rs = lax.rsqrt((jnp.sum(kn * kn, axis=-1, keepdims=True) + kr_ss) * (1.0 / QK_DIM) + EPS)
            k_ref[b, hd, :, :LANE] = (kn * rs * gkn[:, :LANE]).astype(BF16)
            k_ref[b, hd, :, LANE:] = (krg * rs).astype(BF16)
            v_ref[b, hd, :, :] = kv[:, hd * HEAD_PAD + LANE:(hd + 1) * HEAD_PAD].astype(BF16)


def _cache_kv(cache_ckv, cache_kr_pad, W):
    return pl.pallas_call(
        _kvc_kernel,
        grid=(DEPTH,),
        in_specs=[
            pl.BlockSpec((DEC_BATCH, None, PAST_LEN, KV_LORA), lambda l: (0, l, 0, 0)),
            pl.BlockSpec((DEC_BATCH, None, PAST_LEN, LANE), lambda l: (0, l, 0, 0)),
            pl.BlockSpec((None, KV_LORA, N_HEADS * HEAD_PAD), lambda l: (l, 0, 0)),
            pl.BlockSpec((None, 1, HEAD_PAD), lambda l: (l, 0, 0)),
        ],
        out_specs=[
            pl.BlockSpec((None, DEC_BATCH, N_HEADS, PAST_LEN, HEAD_PAD), lambda l: (l, 0, 0, 0, 0)),
            pl.BlockSpec((None, DEC_BATCH, N_HEADS, PAST_LEN, V_DIM), lambda l: (l, 0, 0, 0, 0)),
        ],
        out_shape=[
            jax.ShapeDtypeStruct((DEPTH, DEC_BATCH, N_HEADS, PAST_LEN, HEAD_PAD), BF16),
            jax.ShapeDtypeStruct((DEPTH, DEC_BATCH, N_HEADS, PAST_LEN, V_DIM), BF16),
        ],
        compiler_params=_params(("parallel",)),
        name="cache_kv",
    )(cache_ckv, cache_kr_pad, W["w_ukv"], W["g_kn"])


def _softmax_pv(q, ks, vs):
    ss = [lax.dot_general(q, k, (((1,), (1,)), ((), ())), preferred_element_type=F32) for k in ks]
    m = functools.reduce(jnp.maximum, [jnp.max(s, axis=-1, keepdims=True) for s in ss])
    ps = [jnp.exp2(s - m) for s in ss]
    den = functools.reduce(jnp.add, [jnp.sum(p, axis=-1, keepdims=True) for p in ps])
    o = functools.reduce(jnp.add, [jnp.dot(p.astype(BF16), v, preferred_element_type=F32) for p, v in zip(ps, vs)])
    return o / den


def _attn_ctx_kernel(q_ref, k_ref, v_ref, o_ref):
    for r0 in range(0, NB_CTX * SEQ, SEQ):
        rows = slice(r0, r0 + SEQ)
        for hd in range(N_HEADS):
            q = q_ref[hd, rows, :]
            k = k_ref[hd, rows, :]
            v = v_ref[hd, rows, :]
            o_ref[rows, hd * V_DIM:(hd + 1) * V_DIM] = _softmax_pv(q, [k], [v]).astype(BF16)


def _attn_ctx(l, q, k, v):
    rows = NB_CTX * SEQ
    return pl.pallas_call(
        _attn_ctx_kernel,
        grid=(BATCH // NB_CTX,),
        in_specs=[
            pl.BlockSpec((N_HEADS, rows, HEAD_PAD), lambda b: (0, b, 0)),
            pl.BlockSpec((N_HEADS, rows, HEAD_PAD), lambda b: (0, b, 0)),
            pl.BlockSpec((N_HEADS, rows, V_DIM), lambda b: (0, b, 0)),
        ],
        out_specs=pl.BlockSpec((rows, N_HEADS * V_DIM), lambda b: (b, 0)),
        out_shape=jax.ShapeDtypeStruct((N_CTX, N_HEADS * V_DIM), BF16),
        compiler_params=_params(("parallel",)),
        name=f"attn_ctx_{l}",
    )(q, k, v)


def _attn_dec_kernel(q_ref, k_ref, v_ref, kc_ref, vc_ref, wg_ref, wu_ref, wd_ref, wo_ref,
                     o_ref, wgb_ref, wub_ref, wdb_ref, wob_ref):
    ks = [k_ref[...], kc_ref[...]]
    vs = [v_ref[...], vc_ref[...]]
    for r in range(0, DEC_SEQ, TQ_DEC):
        o_ref[r:r + TQ_DEC, :] = _softmax_pv(q_ref[r:r + TQ_DEC, :], ks, vs).astype(BF16)
    wgb_ref[...] = wg_ref[...].astype(BF16)
    wub_ref[...] = wu_ref[...].astype(BF16)
    wdb_ref[...] = wd_ref[...].astype(BF16)
    wob_ref[...] = wo_ref[...].astype(BF16)


def _attn_dec(l, q, k, v, kc, vc, w_gate, w_up, w_down, w_out):
    first = N_CTX // DEC_SEQ
    steps = DEC_BATCH * N_HEADS
    slab = lambda rows, cols: pl.BlockSpec((None, rows // steps, cols), lambda b, h: (l, b * N_HEADS + h, 0))
    slab_out = lambda rows, cols: pl.BlockSpec((rows // steps, cols), lambda b, h: (b * N_HEADS + h, 0))
    return pl.pallas_call(
        _attn_dec_kernel,
        grid=(DEC_BATCH, N_HEADS),
        in_specs=[
            pl.BlockSpec((None, DEC_SEQ, HEAD_PAD), lambda b, h: (h, first + b, 0)),
            pl.BlockSpec((None, DEC_SEQ, HEAD_PAD), lambda b, h: (h, first + b, 0)),
            pl.BlockSpec((None, DEC_SEQ, V_DIM), lambda b, h: (h, first + b, 0)),
            pl.BlockSpec((None, None, None, PAST_LEN, HEAD_PAD), lambda b, h: (l, b, h, 0, 0)),
            pl.BlockSpec((None, None, None, PAST_LEN, V_DIM), lambda b, h: (l, b, h, 0, 0)),
            slab(D_MODEL, D_FF), slab(D_MODEL, D_FF), slab(D_FF, D_MODEL), slab(D_MODEL, D_MODEL),
        ],
        out_specs=[pl.BlockSpec((DEC_SEQ, V_DIM), lambda b, h: (b, h)),
                   slab_out(D_MODEL, D_FF), slab_out(D_MODEL, D_FF), slab_out(D_FF, D_MODEL),
                   slab_out(D_MODEL, D_MODEL)],
        out_shape=[jax.ShapeDtypeStruct((N_DEC, N_HEADS * V_DIM), BF16),
                   jax.ShapeDtypeStruct((D_MODEL, D_FF), BF16), jax.ShapeDtypeStruct((D_MODEL, D_FF), BF16),
                   jax.ShapeDtypeStruct((D_FF, D_MODEL), BF16), jax.ShapeDtypeStruct((D_MODEL, D_MODEL), BF16)],
        compiler_params=_params(("parallel", "parallel")),
        name=f"attn_dec_{l}",
    )(q, k, v, kc, vc, w_gate, w_up, w_down, w_out)


def _mix_block(pin_ref, cv_ref, wp_ref, ps_ref, cw_ref, o_ref, pad_ref, seq):
    n_seq = TS_MIX // seq
    zeros = jnp.zeros((n_seq, HALO, LANE), F32)
    pad_ref[:, :HALO, :] = zeros
    pad_ref[:, HALO + seq:, :] = zeros
    pos = lax.broadcasted_iota(jnp.int32, (seq, LANE), 0)

    def put(x):
        pad_ref[:, HALO:HALO + seq, :] = x.reshape(n_seq, seq, LANE)

    def at(j):
        return pad_ref[:, HALO + j:HALO + j + seq, :]

    for g, w in enumerate(POOL_WINDOWS):
        half = w // 2
        u = pin_ref[:, g * POOL_GC:(g + 1) * POOL_GC].astype(F32)
        put(u)
        total = functools.reduce(jnp.add, [at(j) for j in range(-half, half)])
        cnt = (jnp.minimum(pos + half, seq) - jnp.maximum(pos - half, 0)).astype(F32)
        d = (total / cnt).reshape(TS_MIX, LANE) - u
        y = jnp.dot(d.astype(BF16), wp_ref[g], preferred_element_type=F32)
        o_ref[:, g * POOL_GC:(g + 1) * POOL_GC] = (y * ps_ref[:, g * POOL_GC:(g + 1) * POOL_GC]).astype(BF16)

    for g in range(CONV_W // LANE):
        sl = slice(g * LANE, (g + 1) * LANE)
        ch = cv_ref[:, g * LANE:(g + 1) * LANE].astype(F32)
        cb = cv_ref[:, CONV_W + g * LANE:CONV_W + (g + 1) * LANE].astype(F32)
        cc = cv_ref[:, 2 * CONV_W + g * LANE:2 * CONV_W + (g + 1) * LANE].astype(F32)
        put(cc * ch)
        conv = at(-1) * cw_ref[0:1, sl] + at(0) * cw_ref[1:2, sl] + at(1) * cw_ref[2:3, sl]
        o_ref[:, POOL_W + g * LANE:POOL_W + (g + 1) * LANE] = (cb * conv.reshape(TS_MIX, LANE)).astype(BF16)


def _mix_kernel(pin_ref, cv_ref, wp_ref, ps_ref, cw_ref, o_ref, pad_ctx_ref, pad_dec_ref):
    i = pl.program_id(0)

    @pl.when(i < N_CTX // TS_MIX)
    def _():
        _mix_block(pin_ref, cv_ref, wp_ref, ps_ref, cw_ref, o_ref, pad_ctx_ref, SEQ)

    @pl.when(i >= N_CTX // TS_MIX)
    def _():
        _mix_block(pin_ref, cv_ref, wp_ref, ps_ref, cw_ref, o_ref, pad_dec_ref, DEC_SEQ)


def _mixers(l, pin, cv, W):
    return pl.pallas_call(
        _mix_kernel,
        grid=(N_TOK // TS_MIX,),
        in_specs=[
            pl.BlockSpec((TS_MIX, POOL_W), lambda i: (i, 0)),
            pl.BlockSpec((TS_MIX, 3 * CONV_W), lambda i: (i, 0)),
            pl.BlockSpec((None, len(POOL_WINDOWS), POOL_GC, POOL_GC), lambda i: (l, 0, 0, 0)),
            pl.BlockSpec((None, 1, POOL_W), lambda i: (l, 0, 0)),
            pl.BlockSpec((None, 3, CONV_W), lambda i: (l, 0, 0)),
        ],
        out_specs=pl.BlockSpec((TS_MIX, POOL_W + CONV_W), lambda i: (i, 0)),
        out_shape=jax.ShapeDtypeStruct((N_TOK, POOL_W + CONV_W), BF16),
        scratch_shapes=[pltpu.VMEM((TS_MIX // SEQ, SEQ + 2 * HALO, LANE), F32),
                        pltpu.VMEM((TS_MIX // DEC_SEQ, DEC_SEQ + 2 * HALO, LANE), F32)],
        compiler_params=_params(("parallel",), VMEM_LIMIT_BIG),
        name=f"mixers_{l}",
    )(pin, cv, W["w_pool"], W["pool_scale"], W["conv_w"])


def _out_kernel(*refs, n_x):
    x_refs, refs = refs[:n_x], refs[n_x:]
    actx_ref, adec_ref, pc_ref, wo_ref, g1_ref, sh_ref, sc_ref, gffn_ref, o_ref, h_ref = refs
    i = pl.program_id(0)
    att_w = N_HEADS * V_DIM

    def finish(att_ref):
        for r0 in range(0, TM_OUT, SUB_OUT):
            rows = slice(r0, r0 + SUB_OUT)
            y = jnp.dot(att_ref[rows, :], wo_ref[:att_w, :], preferred_element_type=F32)
            y = y + jnp.dot(pc_ref[rows, :], wo_ref[att_w:, :], preferred_element_type=F32)
            x = _token_rows(x_refs, rows, TM_OUT) + g1_ref[...] * y
            o_ref[rows, :] = x
            h = x * _rms(x, D_MODEL) * gffn_ref[...]
            h_ref[rows, :] = (h * (1.0 + sc_ref[...]) + sh_ref[...]).astype(BF16)

    @pl.when(i < N_CTX // TM_OUT)
    def _():
        finish(actx_ref)

    @pl.when(i >= N_CTX // TM_OUT)
    def _():
        finish(adec_ref)


def _out_proj(l, xs, att_ctx, att_dec, pc, w_out_b, mod4, W):
    tm = TM_OUT
    n_ctx_tiles = N_CTX // tm
    row = functools.partial(_mod_row, tm=tm)
    att_w = N_HEADS * V_DIM
    return pl.pallas_call(
        functools.partial(_out_kernel, n_x=len(xs)),
        grid=(N_TOK // tm,),
        in_specs=_token_specs(tm, D_MODEL, len(xs) == 2) + [
            pl.BlockSpec((tm, att_w), lambda i: (jnp.minimum(i, n_ctx_tiles - 1), 0)),
            pl.BlockSpec((tm, att_w), lambda i: (jnp.maximum(i - n_ctx_tiles, 0), 0)),
            pl.BlockSpec((tm, POOL_W + CONV_W), lambda i: (i, 0)),
            pl.BlockSpec((D_MODEL, D_MODEL), lambda i: (0, 0), pipeline_mode=pl.Buffered(1)),
            pl.BlockSpec((None, None, 1, D_MODEL), lambda i: (l, row(i), 0, 2)),
            pl.BlockSpec((None, None, 1, D_MODEL), lambda i: (l, row(i), 0, 3)),
            pl.BlockSpec((None, None, 1, D_MODEL), lambda i: (l, row(i), 0, 4)),
            pl.BlockSpec((None, 1, D_MODEL), lambda i: (l, 0, 0)),
        ],
        out_specs=[pl.BlockSpec((tm, D_MODEL), lambda i: (i, 0)), pl.BlockSpec((tm, D_MODEL), lambda i: (i, 0))],
        out_shape=[jax.ShapeDtypeStruct((N_TOK, D_MODEL), F32), jax.ShapeDtypeStruct((N_TOK, D_MODEL), BF16)],
        compiler_params=_params(("parallel",)),
        name=f"out_proj_{l}",
    )(*xs, att_ctx, att_dec, pc, w_out_b, mod4, mod4, mod4, W["g_ffn"])


def _ffn_kernel(x_ref, h_ref, g2_ref, wg_ref, wu_ref, wd_ref, o_ref):
    f = pl.program_id(1)

    @pl.when(f == 0)
    def _():
        o_ref[...] = x_ref[...]

    h = h_ref[...]
    y = None
    for c0 in range(0, TF_FFN, SUB_FFN):
        gate = jnp.dot(h, wg_ref[:, c0:c0 + SUB_FFN], preferred_element_type=F32)
        up = jnp.dot(h, wu_ref[:, c0:c0 + SUB_FFN], preferred_element_type=F32)
        part = jnp.dot((_silu(gate) * up).astype(BF16), wd_ref[c0:c0 + SUB_FFN, :], preferred_element_type=F32)
        y = part if y is None else y + part
    o_ref[...] += g2_ref[...] * y


def _ffn(l, x_all, h_all, mod4, ffn_w, row0=0, n_rows=N_TOK):
    tm, tf = TM_FFN, TF_FFN
    t0 = row0 // tm
    row = lambda i: _mod_row(i + t0, tm)
    return pl.pallas_call(
        _ffn_kernel,
        grid=(n_rows // tm, D_FF // tf),
        in_specs=[
            pl.BlockSpec((tm, D_MODEL), lambda i, f: (i + t0, 0)),
            pl.BlockSpec((tm, D_MODEL), lambda i, f: (i + t0, 0)),
            pl.BlockSpec((None, None, 1, D_MODEL), lambda i, f: (l, row(i), 0, 5)),
            pl.BlockSpec((D_MODEL, tf), lambda i, f: (0, f)),
            pl.BlockSpec((D_MODEL, tf), lambda i, f: (0, f)),
            pl.BlockSpec((tf, D_MODEL), lambda i, f: (f, 0)),
        ],
        out_specs=pl.BlockSpec((tm, D_MODEL), lambda i, f: (i, 0)),
        out_shape=jax.ShapeDtypeStruct((n_rows, D_MODEL), F32),
        compiler_params=_params(("parallel", "arbitrary"), VMEM_LIMIT_BIG),
        name=f"ffn_{l}_{row0}",
    )(x_all, h_all, mod4, *ffn_w)


def _prep_weights(g_mix, g_ffn, w_in, g_q_a, w_uq, g_kv_a, w_ukv, g_qn, g_kn, w_pool, pool_scale, conv_w):
    w_in_a, w_in_b = _prep_w_in(w_in)
    w_uq_p = jnp.pad(w_uq.reshape(DEPTH, Q_LORA, N_HEADS, QK_DIM), ((0, 0), (0, 0), (0, 0), (0, HEAD_PAD - QK_DIM)))
    head_vec = lambda g: jnp.pad(g, ((0, 0), (0, HEAD_PAD - QK_DIM))).reshape(DEPTH, 1, HEAD_PAD)
    cos, sin = _rope_tables()
    return dict(
        g_mix=g_mix.reshape(DEPTH, 1, D_MODEL), g_ffn=g_ffn.reshape(DEPTH, 1, D_MODEL),
        w_in_a=w_in_a, w_in_b=w_in_b, g_q_a=g_q_a.reshape(DEPTH, 1, Q_LORA),
        w_uq=w_uq_p.reshape(DEPTH, Q_LORA, N_HEADS * HEAD_PAD).astype(BF16),
        g_kv_a=g_kv_a.reshape(DEPTH, 1, KV_LORA), w_ukv=w_ukv.astype(BF16),
        g_qn=head_vec(g_qn), g_kn=head_vec(g_kn), w_pool=w_pool.astype(BF16),
        pool_scale=pool_scale.reshape(DEPTH, 1, POOL_W), conv_w=conv_w, cos=cos, sin=sin)


def kernel(x_prompt, x_sample, cache_ckv, cache_krope, c, c_ctx, w_ada, b_ada, g_mix, g_ffn, w_in, g_q_a, w_uq,
           g_kv_a, w_ukv, g_qn, g_kn, w_pool, pool_scale, conv_w, w_out, w_gate, w_up, w_down):
    W = _prep_weights(g_mix, g_ffn, w_in, g_q_a, w_uq, g_kv_a, w_ukv, g_qn, g_kn, w_pool, pool_scale, conv_w)
    m_all = jnp.concatenate([c_ctx[None, :], c, jnp.zeros((MOD_ROWS - 1 - DEC_BATCH, D_MODEL), F32)], axis=0)
    mod4 = _ada(m_all, w_ada, b_ada).reshape(DEPTH, MOD_ROWS, 1, N_MOD)
    cache_kr_pad = jnp.pad(cache_krope, ((0, 0), (0, 0), (0, 0), (0, LANE - ROPE_DIM)))
    kc, vc = _cache_kv(cache_ckv, cache_kr_pad, W)

    xs = (x_prompt.reshape(N_CTX, D_MODEL), x_sample.reshape(N_DEC, D_MODEL))
    new_ckv, new_kr = [], []
    for l in range(DEPTH):
        q, k, v, ckvn, kr, pin, cv = _in_proj(l, xs, mod4, W)
        att_ctx = _attn_ctx(l, q, k, v)
        att_dec, *ffn_w, w_out_b = _attn_dec(l, q, k, v, kc, vc, w_gate, w_up, w_down, w_out)
        pc = _mixers(l, pin, cv, W)
        x_all, h_all = _out_proj(l, xs, att_ctx, att_dec, pc, w_out_b, mod4, W)
        if l < DEPTH - 1:
            xs = (_ffn(l, x_all, h_all, mod4, ffn_w),)
        else:
            y_ctx = _ffn(l, x_all, h_all, mod4, ffn_w, 0, N_CTX)
            y_dec = _ffn(l, x_all, h_all, mod4, ffn_w, N_CTX, N_DEC)
        new_ckv.append(ckvn[:N_CTX].reshape(BATCH, SEQ, KV_LORA))
        new_kr.append(kr[:N_CTX].reshape(BATCH, SEQ, ROPE_DIM))
    y_prompt = y_ctx.reshape(BATCH, SEQ, D_MODEL)
    y_sample = y_dec.reshape(DEC_BATCH, DEC_SEQ, D_MODEL)
    return y_prompt, y_sample, jnp.stack(new_ckv, axis=1), jnp.stack(new_kr, axis=1)
```

```python
import functools

import jax
import jax.numpy as jnp
import numpy as np
from jax import lax
from jax.experimental import pallas as pl
from jax.experimental.pallas import tpu as pltpu

D_MODEL = 2048
BATCH = 16
SEQ = 256
DEPTH = 4
DEC_BATCH = 4
DEC_SEQ = 2048
PAST_LEN = 512
GRID_W = 64
N_HEADS = 8
QK_NOPE = 128
ROPE_DIM = 64
QK_DIM = QK_NOPE + ROPE_DIM
V_DIM = 128
Q_LORA = 512
KV_LORA = 256
POOL_WINDOWS = (2, 4, 8, 16)
POOL_GC = 128
POOL_W = 512
CONV_W = 512
D_FF = 5632
ROPE_BASE = 10000.0
EPS = 1e-6

N_CTX = BATCH * SEQ
N_DEC = DEC_BATCH * DEC_SEQ
N_TOK = N_CTX + N_DEC
N_MOD = 6 * D_MODEL
MOD_ROWS = 8

LANE = 128
HEAD_PAD = 2 * LANE
IN_W = Q_LORA + KV_LORA + ROPE_DIM + POOL_W + 3 * CONV_W
IN_CQ = 0
IN_CKV = IN_CQ + Q_LORA
IN_KR = IN_CKV + KV_LORA
IN_A = IN_KR + LANE
IN_A_SRC = IN_KR + ROPE_DIM
IN_B = POOL_W + 3 * CONV_W

TM_IN = 512
SUB_IN = 256
TM_OUT = 512
SUB_OUT = 256
TM_FFN = 1024
TF_FFN = 512
SUB_FFN = 256
TN_ADA = 1024
TS_MIX = 2048
HALO = 8
NB_CTX = 8
TQ_DEC = 256
TR_WIN = 512
VMEM_LIMIT = 56 * 1024 * 1024
VMEM_LIMIT_BIG = 62 * 1024 * 1024

F32 = jnp.float32
BF16 = jnp.bfloat16
Q_SCALE = QK_DIM ** -0.5 * 1.4426950408889634


def _mod_row(i, tm):
    n_ctx_tiles = N_CTX // tm
    return jnp.where(i < n_ctx_tiles, 0, 1 + (i - n_ctx_tiles) // (DEC_SEQ // tm))


def _params(sem, vmem=VMEM_LIMIT):
    return pltpu.CompilerParams(dimension_semantics=sem, vmem_limit_bytes=vmem)


def _silu(x):
    return x / (1.0 + jnp.exp(-x))


def _rms(x, n):
    return lax.rsqrt(jnp.sum(x * x, axis=-1, keepdims=True) * (1.0 / n) + EPS)


def _token_specs(tm, width, split):
    if not split:
        return [pl.BlockSpec((tm, width), lambda i, *_: (i, 0))]
    n_ctx_tiles = N_CTX // tm
    return [pl.BlockSpec((tm, width), lambda i, *_: (jnp.minimum(i, n_ctx_tiles - 1), 0)),
            pl.BlockSpec((tm, width), lambda i, *_: (jnp.maximum(i - n_ctx_tiles, 0), 0))]


def _token_rows(x_refs, rows, tm):
    if len(x_refs) == 1:
        return x_refs[0][rows, :]
    return jnp.where(pl.program_id(0) < N_CTX // tm, x_refs[0][rows, :], x_refs[1][rows, :])


def _ada_kernel(m_ref, w_ref, b_ref, o_ref):
    a = _silu(m_ref[...]).astype(BF16)
    o_ref[...] = jnp.dot(a, w_ref[...].astype(BF16), preferred_element_type=F32) + b_ref[...]


def _ada(m_all, w_ada, b_ada):
    return pl.pallas_call(
        _ada_kernel,
        grid=(DEPTH, N_MOD // TN_ADA),
        in_specs=[
            pl.BlockSpec((MOD_ROWS, D_MODEL), lambda l, j: (0, 0)),
            pl.BlockSpec((None, D_MODEL, TN_ADA), lambda l, j: (l, 0, j)),
            pl.BlockSpec((None, 1, TN_ADA), lambda l, j: (l, 0, j)),
        ],
        out_specs=pl.BlockSpec((None, MOD_ROWS, TN_ADA), lambda l, j: (l, 0, j)),
        out_shape=jax.ShapeDtypeStruct((DEPTH, MOD_ROWS, N_MOD), F32),
        compiler_params=_params(("parallel", "parallel")),
        name="ada_mod",
    )(m_all, w_ada, b_ada.reshape(DEPTH, 1, N_MOD))


def _w_in_kernel(wt_ref, a_ref, b_ref):
    a = wt_ref[:IN_A, :].T
    lane = lax.broadcasted_iota(jnp.int32, a.shape, 1)
    a_ref[...] = jnp.where(lane < IN_A_SRC, a, 0.0).astype(BF16)
    b_ref[...] = wt_ref[IN_A_SRC:, :].T.astype(BF16)


def _prep_w_in(w_in):
    return pl.pallas_call(
        _w_in_kernel,
        grid=(DEPTH, D_MODEL // TR_WIN),
        in_specs=[pl.BlockSpec((None, IN_W, TR_WIN), lambda l, r: (l, 0, r))],
        out_specs=[pl.BlockSpec((None, TR_WIN, IN_A), lambda l, r: (l, r, 0)),
                   pl.BlockSpec((None, TR_WIN, IN_B), lambda l, r: (l, r, 0))],
        out_shape=[jax.ShapeDtypeStruct((DEPTH, D_MODEL, IN_A), BF16),
                   jax.ShapeDtypeStruct((DEPTH, D_MODEL, IN_B), BF16)],
        compiler_params=_params(("parallel", "parallel")),
        name="prep_w_in",
    )(jnp.swapaxes(w_in, 1, 2))


def _rope_tables():
    f32 = np.float32
    t = np.arange(DEC_SEQ)
    r_pos = (t // GRID_W).astype(f32)
    c_pos = (t % GRID_W).astype(f32)
    nf = ROPE_DIM // 4
    inv = f32(ROPE_BASE) ** (-np.arange(nf, dtype=f32) / f32(nf))
    ang_r = r_pos[:, None] * inv[None, :]
    ang_c = c_pos[:, None] * inv[None, :]
    zeros = np.zeros((DEC_SEQ, LANE - ROPE_DIM), f32)
    cos = np.concatenate([np.cos(ang_r), np.cos(ang_r), np.cos(ang_c), np.cos(ang_c), zeros], axis=1)
    sin = np.concatenate([-np.sin(ang_r), np.sin(ang_r), -np.sin(ang_c), np.sin(ang_c), zeros], axis=1)
    ident_cos = np.concatenate([np.ones((TM_IN, ROPE_DIM), f32), np.zeros((TM_IN, LANE - ROPE_DIM), f32)], axis=1)
    ident_sin = np.zeros((TM_IN, LANE), f32)
    return (jnp.asarray(np.concatenate([ident_cos, cos], axis=0).astype(f32)),
            jnp.asarray(np.concatenate([ident_sin, sin], axis=0).astype(f32)))


def _rope(x, cos, sin, first_half):
    n = x.shape[-1]
    swapped = jnp.where(first_half, pltpu.roll(x, n - ROPE_DIM // 4, 1), pltpu.roll(x, ROPE_DIM // 4, 1))
    return x * cos + swapped * sin


def _in_kernel(*refs, n_x):
    x_refs, refs = refs[:n_x], refs[n_x:]
    (sh_ref, sc_ref, gmix_ref, wa_ref, wb_ref, gqa_ref, wuq_ref, gqn_ref, gkva_ref, wukv_ref, gkn_ref,
     cos_ref, sin_ref, q_ref, k_ref, v_ref, ckv_ref, kr_ref, pin_ref, cv_ref, hb_scr) = refs
    lane = lax.broadcasted_iota(jnp.int32, (SUB_IN, LANE), 1)
    first_half = (lane & (ROPE_DIM // 4)) == 0
    gqn = gqn_ref[...]
    gkn = gkn_ref[...]

    for r0 in range(0, TM_IN, SUB_IN):
        rows = slice(r0, r0 + SUB_IN)
        x = _token_rows(x_refs, rows, TM_IN)
        h = x * _rms(x, D_MODEL) * gmix_ref[...]
        h = h * (1.0 + sc_ref[...]) + sh_ref[...]
        hb = h.astype(BF16)
        hb_scr[rows, :] = hb
        u = jnp.dot(hb, wa_ref[...], preferred_element_type=F32)
        cos = cos_ref[rows, :]
        sin = sin_ref[rows, :]

        cq = u[:, IN_CQ:IN_CQ + Q_LORA]
        cqn = cq * _rms(cq, Q_LORA) * gqa_ref[...]
        q = jnp.dot(cqn.astype(BF16), wuq_ref[...], preferred_element_type=F32)
        for hd in range(N_HEADS):
            qh = q[:, hd * HEAD_PAD:(hd + 1) * HEAD_PAD]
            qh = qh * (_rms(qh, QK_DIM) * Q_SCALE) * gqn
            q_ref[hd, rows, :LANE] = qh[:, :LANE].astype(BF16)
            q_ref[hd, rows, LANE:] = (
                _rope(qh[:, LANE:], cos, sin, first_half).astype(BF16))

        ckv = u[:, IN_CKV:IN_CKV + KV_LORA]
        ckvn = ckv * _rms(ckv, KV_LORA) * gkva_ref[...]
        ckv_ref[rows, :] = ckvn
        kv = jnp.dot(ckvn.astype(BF16), wukv_ref[...], preferred_element_type=F32)
        krb = u[:, IN_KR:IN_KR + LANE]
        kr_ref[rows, :] = krb[:, :ROPE_DIM]
        kr_ss = jnp.sum(krb * krb, axis=-1, keepdims=True)
        krg = _rope(krb * gkn[:, LANE:], cos, sin, first_half)
        for hd in range(N_HEADS):
            kn = kv[:, hd * HEAD_PAD:hd * HEAD_PAD + LANE]
            rs = lax.rsqrt((jnp.sum(kn * kn, axis=-1, keepdims=True) + kr_ss) * (1.0 / QK_DIM) + EPS)
            k_ref[hd, rows, :LANE] = (kn * rs * gkn[:, :LANE]).astype(BF16)
            k_ref[hd, rows, LANE:] = (krg * rs).astype(BF16)
            v_ref[hd, rows, :] = kv[:, hd * HEAD_PAD + LANE:(hd + 1) * HEAD_PAD].astype(BF16)

    um = jnp.dot(hb_scr[...], wb_ref[...], preferred_element_type=F32)
    pin_ref[...] = um[:, :POOL_W].astype(BF16)
    cv_ref[...] = um[:, POOL_W:].astype(BF16)


def _in_proj(l, xs, mod4, W):
    tm = TM_IN
    n_ctx_tiles = N_CTX // tm
    row = functools.partial(_mod_row, tm=tm)
    vec = lambda w: pl.BlockSpec((None, 1, w), lambda i: (l, 0, 0))
    mat = lambda r, c: pl.BlockSpec((None, r, c), lambda i: (l, 0, 0), pipeline_mode=pl.Buffered(1))
    tab = pl.BlockSpec((tm, LANE), lambda i: (jnp.where(i < n_ctx_tiles, 0, 1 + (i - n_ctx_tiles) % (DEC_SEQ // tm)), 0))
    tok = lambda w: pl.BlockSpec((tm, w), lambda i: (i, 0))
    head = lambda w: pl.BlockSpec((N_HEADS, tm, w), lambda i: (0, i, 0))
    return pl.pallas_call(
        functools.partial(_in_kernel, n_x=len(xs)),
        grid=(N_TOK // tm,),
        in_specs=_token_specs(tm, D_MODEL, len(xs) == 2) + [
            pl.BlockSpec((None, None, 1, D_MODEL), lambda i: (l, row(i), 0, 0)),
            pl.BlockSpec((None, None, 1, D_MODEL), lambda i: (l, row(i), 0, 1)),
            vec(D_MODEL), mat(D_MODEL, IN_A), mat(D_MODEL, IN_B), vec(Q_LORA), mat(Q_LORA, N_HEADS * HEAD_PAD),
            vec(HEAD_PAD),
            vec(KV_LORA), mat(KV_LORA, N_HEADS * HEAD_PAD), vec(HEAD_PAD), tab, tab,
        ],
        out_specs=[head(HEAD_PAD), head(HEAD_PAD), head(V_DIM), tok(KV_LORA),
                   tok(ROPE_DIM), tok(POOL_W), tok(3 * CONV_W)],
        out_shape=[
            jax.ShapeDtypeStruct((N_HEADS, N_TOK, HEAD_PAD), BF16),
            jax.ShapeDtypeStruct((N_HEADS, N_TOK, HEAD_PAD), BF16),
            jax.ShapeDtypeStruct((N_HEADS, N_TOK, V_DIM), BF16),
            jax.ShapeDtypeStruct((N_TOK, KV_LORA), F32),
            jax.ShapeDtypeStruct((N_TOK, ROPE_DIM), F32),
            jax.ShapeDtypeStruct((N_TOK, POOL_W), BF16),
            jax.ShapeDtypeStruct((N_TOK, 3 * CONV_W), BF16),
        ],
        scratch_shapes=[pltpu.VMEM((tm, D_MODEL), BF16)],
        compiler_params=_params(("parallel",), VMEM_LIMIT_BIG if len(xs) == 2 else VMEM_LIMIT),
        name=f"in_proj_{l}",
    )(*xs, mod4, mod4, W["g_mix"], W["w_in_a"], W["w_in_b"], W["g_q_a"], W["w_uq"], W["g_qn"], W["g_kv_a"],
      W["w_ukv"], W["g_kn"], W["cos"], W["sin"])


def _kvc_kernel(ckv_ref, kr_ref, wukv_ref, gkn_ref, k_ref, v_ref):
    gkn = gkn_ref[...]
    for b in range(DEC_BATCH):
        kv = jnp.dot(ckv_ref[b].astype(BF16), wukv_ref[...], preferred_element_type=F32)
        krb = kr_ref[b]
        kr_ss = jnp.sum(krb * krb, axis=-1, keepdims=True)
        krg = krb * gkn[:, LANE:]
        for hd in range(N_HEADS):
            kn = kv[:, hd * HEAD_PAD:hd * HEAD_PAD + LANE]
            rs = lax.rsqrt((jnp.sum(kn * kn, axis=-1, keepdims=True) + kr_ss) * (1.0 / QK_DIM) + EPS)
            k_ref[b, hd, :, :LANE] = (kn * rs * gkn[:, :LANE]).astype(BF16)
            k_ref[b, hd, :, LANE:] = (krg * rs).astype(BF16)
            v_ref[b, hd, :, :] = kv[:, hd * HEAD_PAD + LANE:(hd + 1) * HEAD_PAD].astype(BF16)


def _cache_kv(cache_ckv, cache_kr_pad, W):
    return pl.pallas_call(
        _kvc_kernel,
        grid=(DEPTH,),
        in_specs=[
            pl.BlockSpec((DEC_BATCH, None, PAST_LEN, KV_LORA), lambda l: (0, l, 0, 0)),
            pl.BlockSpec((DEC_BATCH, None, PAST_LEN, LANE), lambda l: (0, l, 0, 0)),
            pl.BlockSpec((None, KV_LORA, N_HEADS * HEAD_PAD), lambda l: (l, 0, 0)),
            pl.BlockSpec((None, 1, HEAD_PAD), lambda l: (l, 0, 0)),
        ],
        out_specs=[
            pl.BlockSpec((None, DEC_BATCH, N_HEADS, PAST_LEN, HEAD_PAD), lambda l: (l, 0, 0, 0, 0)),
            pl.BlockSpec((None, DEC_BATCH, N_HEADS, PAST_LEN, V_DIM), lambda l: (l, 0, 0, 0, 0)),
        ],
        out_shape=[
            jax.ShapeDtypeStruct((DEPTH, DEC_BATCH, N_HEADS, PAST_LEN, HEAD_PAD), BF16),
            jax.ShapeDtypeStruct((DEPTH, DEC_BATCH, N_HEADS, PAST_LEN, V_DIM), BF16),
        ],
        compiler_params=_params(("parallel",)),
        name="cache_kv",
    )(cache_ckv, cache_kr_pad, W["w_ukv"], W["g_kn"])


def _softmax_pv(q, ks, vs):
    ss = [lax.dot_general(q, k, (((1,), (1,)), ((), ())), preferred_element_type=F32) for k in ks]
    m = functools.reduce(jnp.maximum, [jnp.max(s, axis=-1, keepdims=True) for s in ss])
    ps = [jnp.exp2(s - m) for s in ss]
    den = functools.reduce(jnp.add, [jnp.sum(p, axis=-1, keepdims=True) for p in ps])
    o = functools.reduce(jnp.add, [jnp.dot(p.astype(BF16), v, preferred_element_type=F32) for p, v in zip(ps, vs)])
    return o / den


def _attn_ctx_kernel(q_ref, k_ref, v_ref, o_ref):
    for r0 in range(0, NB_CTX * SEQ, SEQ):
        rows = slice(r0, r0 + SEQ)
        for hd in range(N_HEADS):
            q = q_ref[hd, rows, :]
            k = k_ref[hd, rows, :]
            v = v_ref[hd, rows, :]
            o_ref[rows, hd * V_DIM:(hd + 1) * V_DIM] = _softmax_pv(q, [k], [v]).astype(BF16)


def _attn_ctx(l, q, k, v):
    rows = NB_CTX * SEQ
    return pl.pallas_call(
        _attn_ctx_kernel,
        grid=(BATCH // NB_CTX,),
        in_specs=[
            pl.BlockSpec((N_HEADS, rows, HEAD_PAD), lambda b: (0, b, 0)),
            pl.BlockSpec((N_HEADS, rows, HEAD_PAD), lambda b: (0, b, 0)),
            pl.BlockSpec((N_HEADS, rows, V_DIM), lambda b: (0, b, 0)),
        ],
        out_specs=pl.BlockSpec((rows, N_HEADS * V_DIM), lambda b: (b, 0)),
        out_shape=jax.ShapeDtypeStruct((N_CTX, N_HEADS * V_DIM), BF16),
        compiler_params=_params(("parallel",)),
        name=f"attn_ctx_{l}",
    )(q, k, v)


def _attn_dec_kernel(q_ref, k_ref, v_ref, kc_ref, vc_ref, wg_ref, wu_ref, wd_ref, wo_ref,
                     o_ref, wgb_ref, wub_ref, wdb_ref, wob_ref):
    ks = [k_ref[...], kc_ref[...]]
    vs = [v_ref[...], vc_ref[...]]
    for r in range(0, DEC_SEQ, TQ_DEC):
        o_ref[r:r + TQ_DEC, :] = _softmax_pv(q_ref[r:r + TQ_DEC, :], ks, vs).astype(BF16)
    wgb_ref[...] = wg_ref[...].astype(BF16)
    wub_ref[...] = wu_ref[...].astype(BF16)
    wdb_ref[...] = wd_ref[...].astype(BF16)
    wob_ref[...] = wo_ref[...].astype(BF16)


def _attn_dec(l, q, k, v, kc, vc, w_gate, w_up, w_down, w_out):
    first = N_CTX // DEC_SEQ
    steps = DEC_BATCH * N_HEADS
    slab = lambda rows, cols: pl.BlockSpec((None, rows // steps, cols), lambda b, h: (l, b * N_HEADS + h, 0))
    slab_out = lambda rows, cols: pl.BlockSpec((rows // steps, cols), lambda b, h: (b * N_HEADS + h, 0))
    return pl.pallas_call(
        _attn_dec_kernel,
        grid=(DEC_BATCH, N_HEADS),
        in_specs=[
            pl.BlockSpec((None, DEC_SEQ, HEAD_PAD), lambda b, h: (h, first + b, 0)),
            pl.BlockSpec((None, DEC_SEQ, HEAD_PAD), lambda b, h: (h, first + b, 0)),
            pl.BlockSpec((None, DEC_SEQ, V_DIM), lambda b, h: (h, first + b, 0)),
            pl.BlockSpec((None, None, None, PAST_LEN, HEAD_PAD), lambda b, h: (l, b, h, 0, 0)),
            pl.BlockSpec((None, None, None, PAST_LEN, V_DIM), lambda b, h: (l, b, h, 0, 0)),
            slab(D_MODEL, D_FF), slab(D_MODEL, D_FF), slab(D_FF, D_MODEL), slab(D_MODEL, D_MODEL),
        ],
        out_specs=[pl.BlockSpec((DEC_SEQ, V_DIM), lambda b, h: (b, h)),
                   slab_out(D_MODEL, D_FF), slab_out(D_MODEL, D_FF), slab_out(D_FF, D_MODEL),
                   slab_out(D_MODEL, D_MODEL)],
        out_shape=[jax.ShapeDtypeStruct((N_DEC, N_HEADS * V_DIM), BF16),
                   jax.ShapeDtypeStruct((D_MODEL, D_FF), BF16), jax.ShapeDtypeStruct((D_MODEL, D_FF), BF16),
                   jax.ShapeDtypeStruct((D_FF, D_MODEL), BF16), jax.ShapeDtypeStruct((D_MODEL, D_MODEL), BF16)],
        compiler_params=_params(("parallel", "parallel")),
        name=f"attn_dec_{l}",
    )(q, k, v, kc, vc, w_gate, w_up, w_down, w_out)


def _mix_block(pin_ref, cv_ref, wp_ref, ps_ref, cw_ref, o_ref, pad_ref, seq):
    n_seq = TS_MIX // seq
    zeros = jnp.zeros((n_seq, HALO, LANE), F32)
    pad_ref[:, :HALO, :] = zeros
    pad_ref[:, HALO + seq:, :] = zeros
    pos = lax.broadcasted_iota(jnp.int32, (seq, LANE), 0)

    def put(x):
        pad_ref[:, HALO:HALO + seq, :] = x.reshape(n_seq, seq, LANE)

    def at(j):
        return pad_ref[:, HALO + j:HALO + j + seq, :]

    for g, w in enumerate(POOL_WINDOWS):
        half = w // 2
        u = pin_ref[:, g * POOL_GC:(g + 1) * POOL_GC].astype(F32)
        put(u)
        total = functools.reduce(jnp.add, [at(j) for j in range(-half, half)])
        cnt = (jnp.minimum(pos + half, seq) - jnp.maximum(pos - half, 0)).astype(F32)
        d = (total / cnt).reshape(TS_MIX, LANE) - u
        y = jnp.dot(d.astype(BF16), wp_ref[g], preferred_element_type=F32)
        o_ref[:, g * POOL_GC:(g + 1) * POOL_GC] = (y * ps_ref[:, g * POOL_GC:(g + 1) * POOL_GC]).astype(BF16)

    for g in range(CONV_W // LANE):
        sl = slice(g * LANE, (g + 1) * LANE)
        ch = cv_ref[:, g * LANE:(g + 1) * LANE].astype(F32)
        cb = cv_ref[:, CONV_W + g * LANE:CONV_W + (g + 1) * LANE].astype(F32)
        cc = cv_ref[:, 2 * CONV_W + g * LANE:2 * CONV_W + (g + 1) * LANE].astype(F32)
        put(cc * ch)
        conv = at(-1) * cw_ref[0:1, sl] + at(0) * cw_ref[1:2, sl] + at(1) * cw_ref[2:3, sl]
        o_ref[:, POOL_W + g * LANE:POOL_W + (g + 1) * LANE] = (cb * conv.reshape(TS_MIX, LANE)).astype(BF16)


def _mix_kernel(pin_ref, cv_ref, wp_ref, ps_ref, cw_ref, o_ref, pad_ctx_ref, pad_dec_ref):
    i = pl.program_id(0)

    @pl.when(i < N_CTX // TS_MIX)
    def _():
        _mix_block(pin_ref, cv_ref, wp_ref, ps_ref, cw_ref, o_ref, pad_ctx_ref, SEQ)

    @pl.when(i >= N_CTX // TS_MIX)
    def _():
        _mix_block(pin_ref, cv_ref, wp_ref, ps_ref, cw_ref, o_ref, pad_dec_ref, DEC_SEQ)


def _mixers(l, pin, cv, W):
    return pl.pallas_call(
        _mix_kernel,
        grid=(N_TOK // TS_MIX,),
        in_specs=[
            pl.BlockSpec((TS_MIX, POOL_W), lambda i: (i, 0)),
            pl.BlockSpec((TS_MIX, 3 * CONV_W), lambda i: (i, 0)),
            pl.BlockSpec((None, len(POOL_WINDOWS), POOL_GC, POOL_GC), lambda i: (l, 0, 0, 0)),
            pl.BlockSpec((None, 1, POOL_W), lambda i: (l, 0, 0)),
            pl.BlockSpec((None, 3, CONV_W), lambda i: (l, 0, 0)),
        ],
        out_specs=pl.BlockSpec((TS_MIX, POOL_W + CONV_W), lambda i: (i, 0)),
        out_shape=jax.ShapeDtypeStruct((N_TOK, POOL_W + CONV_W), BF16),
        scratch_shapes=[pltpu.VMEM((TS_MIX // SEQ, SEQ + 2 * HALO, LANE), F32),
                        pltpu.VMEM((TS_MIX // DEC_SEQ, DEC_SEQ + 2 * HALO, LANE), F32)],
        compiler_params=_params(("parallel",), VMEM_LIMIT_BIG),
        name=f"mixers_{l}",
    )(pin, cv, W["w_pool"], W["pool_scale"], W["conv_w"])


def _out_kernel(*refs, n_x):
    x_refs, refs = refs[:n_x], refs[n_x:]
    actx_ref, adec_ref, pc_ref, wo_ref, g1_ref, sh_ref, sc_ref, gffn_ref, o_ref, h_ref = refs
    i = pl.program_id(0)
    att_w = N_HEADS * V_DIM

    def finish(att_ref):
        for r0 in range(0, TM_OUT, SUB_OUT):
            rows = slice(r0, r0 + SUB_OUT)
            y = jnp.dot(att_ref[rows, :], wo_ref[:att_w, :], preferred_element_type=F32)
            y = y + jnp.dot(pc_ref[rows, :], wo_ref[att_w:, :], preferred_element_type=F32)
            x = _token_rows(x_refs, rows, TM_OUT) + g1_ref[...] * y
            o_ref[rows, :] = x
            h = x * _rms(x, D_MODEL) * gffn_ref[...]
            h_ref[rows, :] = (h * (1.0 + sc_ref[...]) + sh_ref[...]).astype(BF16)

    @pl.when(i < N_CTX // TM_OUT)
    def _():
        finish(actx_ref)

    @pl.when(i >= N_CTX // TM_OUT)
    def _():
        finish(adec_ref)


def _out_proj(l, xs, att_ctx, att_dec, pc, w_out_b, mod4, W):
    tm = TM_OUT
    n_ctx_tiles = N_CTX // tm
    row = functools.partial(_mod_row, tm=tm)
    att_w = N_HEADS * V_DIM
    return pl.pallas_call(
        functools.partial(_out_kernel, n_x=len(xs)),
        grid=(N_TOK // tm,),
        in_specs=_token_specs(tm, D_MODEL, len(xs) == 2) + [
            pl.BlockSpec((tm, att_w), lambda i: (jnp.minimum(i, n_ctx_tiles - 1), 0)),
            pl.BlockSpec((tm, att_w), lambda i: (jnp.maximum(i - n_ctx_tiles, 0), 0)),
            pl.BlockSpec((tm, POOL_W + CONV_W), lambda i: (i, 0)),
            pl.BlockSpec((D_MODEL, D_MODEL), lambda i: (0, 0), pipeline_mode=pl.Buffered(1)),
            pl.BlockSpec((None, None, 1, D_MODEL), lambda i: (l, row(i), 0, 2)),
            pl.BlockSpec((None, None, 1, D_MODEL), lambda i: (l, row(i), 0, 3)),
            pl.BlockSpec((None, None, 1, D_MODEL), lambda i: (l, row(i), 0, 4)),
            pl.BlockSpec((None, 1, D_MODEL), lambda i: (l, 0, 0)),
        ],
        out_specs=[pl.BlockSpec((tm, D_MODEL), lambda i: (i, 0)), pl.BlockSpec((tm, D_MODEL), lambda i: (i, 0))],
        out_shape=[jax.ShapeDtypeStruct((N_TOK, D_MODEL), F32), jax.ShapeDtypeStruct((N_TOK, D_MODEL), BF16)],
        compiler_params=_params(("parallel",)),
        name=f"out_proj_{l}",
    )(*xs, att_ctx, att_dec, pc, w_out_b, mod4, mod4, mod4, W["g_ffn"])


def _ffn_kernel(x_ref, h_ref, g2_ref, wg_ref, wu_ref, wd_ref, o_ref):
    f = pl.program_id(1)

    @pl.when(f == 0)
    def _():
        o_ref[...] = x_ref[...]

    h = h_ref[...]
    y = None
    for c0 in range(0, TF_FFN, SUB_FFN):
        gate = jnp.dot(h, wg_ref[:, c0:c0 + SUB_FFN], preferred_element_type=F32)
        up = jnp.dot(h, wu_ref[:, c0:c0 + SUB_FFN], preferred_element_type=F32)
        part = jnp.dot((_silu(gate) * up).astype(BF16), wd_ref[c0:c0 + SUB_FFN, :], preferred_element_type=F32)
        y = part if y is None else y + part
    o_ref[...] += g2_ref[...] * y


def _ffn(l, x_all, h_all, mod4, ffn_w, row0=0, n_rows=N_TOK):
    tm, tf = TM_FFN, TF_FFN
    t0 = row0 // tm
    row = lambda i: _mod_row(i + t0, tm)
    return pl.pallas_call(
        _ffn_kernel,
        grid=(n_rows // tm, D_FF // tf),
        in_specs=[
            pl.BlockSpec((tm, D_MODEL), lambda i, f: (i + t0, 0)),
            pl.BlockSpec((tm, D_MODEL), lambda i, f: (i + t0, 0)),
            pl.BlockSpec((None, None, 1, D_MODEL), lambda i, f: (l, row(i), 0, 5)),
            pl.BlockSpec((D_MODEL, tf), lambda i, f: (0, f)),
            pl.BlockSpec((D_MODEL, tf), lambda i, f: (0, f)),
            pl.BlockSpec((tf, D_MODEL), lambda i, f: (f, 0)),
        ],
        out_specs=pl.BlockSpec((tm, D_MODEL), lambda i, f: (i, 0)),
        out_shape=jax.ShapeDtypeStruct((n_rows, D_MODEL), F32),
        compiler_params=_params(("parallel", "arbitrary"), VMEM_LIMIT_BIG),
        name=f"ffn_{l}_{row0}",
    )(x_all, h_all, mod4, *ffn_w)


def _prep_weights(g_mix, g_ffn, w_in, g_q_a, w_uq, g_kv_a, w_ukv, g_qn, g_kn, w_pool, pool_scale, conv_w):
    w_in_a, w_in_b = _prep_w_in(w_in)
    w_uq_p = jnp.pad(w_uq.reshape(DEPTH, Q_LORA, N_HEADS, QK_DIM), ((0, 0), (0, 0), (0, 0), (0, HEAD_PAD - QK_DIM)))
    head_vec = lambda g: jnp.pad(g, ((0, 0), (0, HEAD_PAD - QK_DIM))).reshape(DEPTH, 1, HEAD_PAD)
    cos, sin = _rope_tables()
    return dict(
        g_mix=g_mix.reshape(DEPTH, 1, D_MODEL), g_ffn=g_ffn.reshape(DEPTH, 1, D_MODEL),
        w_in_a=w_in_a, w_in_b=w_in_b, g_q_a=g_q_a.reshape(DEPTH, 1, Q_LORA),
        w_uq=w_uq_p.reshape(DEPTH, Q_LORA, N_HEADS * HEAD_PAD).astype(BF16),
        g_kv_a=g_kv_a.reshape(DEPTH, 1, KV_LORA), w_ukv=w_ukv.astype(BF16),
        g_qn=head_vec(g_qn), g_kn=head_vec(g_kn), w_pool=w_pool.astype(BF16),
        pool_scale=pool_scale.reshape(DEPTH, 1, POOL_W), conv_w=conv_w, cos=cos, sin=sin)


def kernel(x_prompt, x_sample, cache_ckv, cache_krope, c, c_ctx, w_ada, b_ada, g_mix, g_ffn, w_in, g_q_a, w_uq,
           g_kv_a, w_ukv, g_qn, g_kn, w_pool, pool_scale, conv_w, w_out, w_gate, w_up, w_down):
    W = _prep_weights(g_mix, g_ffn, w_in, g_q_a, w_uq, g_kv_a, w_ukv, g_qn, g_kn, w_pool, pool_scale, conv_w)
    m_all = jnp.concatenate([c_ctx[None, :], c, jnp.zeros((MOD_ROWS - 1 - DEC_BATCH, D_MODEL), F32)], axis=0)
    mod4 = _ada(m_all, w_ada, b_ada).reshape(DEPTH, MOD_ROWS, 1, N_MOD)
    cache_kr_pad = jnp.pad(cache_krope, ((0, 0), (0, 0), (0, 0), (0, LANE - ROPE_DIM)))
    kc, vc = _cache_kv(cache_ckv, cache_kr_pad, W)

    xs = (x_prompt.reshape(N_CTX, D_MODEL), x_sample.reshape(N_DEC, D_MODEL))
    new_ckv, new_kr = [], []
    for l in range(DEPTH):
        q, k, v, ckvn, kr, pin, cv = _in_proj(l, xs, mod4, W)
        att_ctx = _attn_ctx(l, q, k, v)
        att_dec, *ffn_w, w_out_b = _attn_dec(l, q, k, v, kc, vc, w_gate, w_up, w_down, w_out)
        pc = _mixers(l, pin, cv, W)
        x_all, h_all = _out_proj(l, xs, att_ctx, att_dec, pc, w_out_b, mod4, W)
        if l < DEPTH - 1:
            xs = (_ffn(l, x_all, h_all, mod4, ffn_w),)
        else:
            y_ctx = _ffn(l, x_all, h_all, mod4, ffn_w, 0, N_CTX)
            y_dec = _ffn(l, x_all, h_all, mod4, ffn_w, N_CTX, N_DEC)
        new_ckv.append(ckvn[:N_CTX].reshape(BATCH, SEQ, KV_LORA))
        new_kr.append(kr[:N_CTX].reshape(BATCH, SEQ, ROPE_DIM))
    y_prompt = y_ctx.reshape(BATCH, SEQ, D_MODEL)
    y_sample = y_dec.reshape(DEC_BATCH, DEC_SEQ, D_MODEL)
    return y_prompt, y_sample, jnp.stack(new_ckv, axis=1), jnp.stack(new_kr, axis=1)
```
